```python
import jax, jax.numpy as jnp
from jax import lax
import numpy as np

D_MODEL = 2048
BATCH = 4
SEQ = 4096
DEPTH = 1

CHUNK = 64
Q_BLOCK = 2 * CHUNK
MIX_WIDTH = D_MODEL
ATTN_WIDTH = MIX_WIDTH // 2
ATTN_HEADS = 8
ATTN_HEAD_DIM = ATTN_WIDTH // ATTN_HEADS
LRU_WIDTH = MIX_WIDTH - ATTN_WIDTH
LRU_BLOCKS = 8
LRU_BLOCK_DIM = LRU_WIDTH // LRU_BLOCKS
CONV_WIDTH = 4
LRU_C = 8.0
OFF_Q = 0
OFF_K = OFF_Q + ATTN_WIDTH
OFF_V = OFF_K + ATTN_WIDTH
OFF_F = OFF_V + ATTN_WIDTH
OFF_LX = OFF_F + ATTN_HEADS
OFF_LY = OFF_LX + LRU_WIDTH
IN_COLS = OFF_LY + LRU_WIDTH
PEER_HEADS = 8
PEER_N_KEYS = 128
PEER_N_EXPERTS = PEER_N_KEYS * PEER_N_KEYS
PEER_QUERY_DIM = 256
PEER_HALF = PEER_QUERY_DIM // 2
PEER_TOPK = 16
PEER_TOKEN_BLOCK = 128
N_ADA = 6
EPS = 1e-6

kernel_name = 'hymba_fox_rglru_peer_adaln'


def _rmsnorm(x, g):
    xf = x.astype(jnp.float32)
    y = xf * lax.rsqrt(jnp.mean(xf * xf, axis=-1, keepdims=True) + EPS)
    return y.astype(x.dtype) * g


def _forgetting_attention(q, k, v, log_f):
    n_blocks = q.shape[2] // Q_BLOCK
    cum = jnp.cumsum(log_f, axis=-1)
    scale = ATTN_HEAD_DIM ** -0.5
    outs = []
    for blk in range(n_blocks):
        start = blk * Q_BLOCK
        end = start + Q_BLOCK
        qb = q[:, :, start:end]
        kb = k[:, :, :end]
        vb = v[:, :, :end]
        logits = jnp.einsum('bhqd,bhkd->bhqk', qb, kb).astype(jnp.float32) * scale
        logits = logits + cum[:, :, start:end, None] - cum[:, :, None, :end]
        mask = jnp.arange(end)[None, :] <= (start + jnp.arange(Q_BLOCK))[:, None]
        logits = jnp.where(mask, logits, -jnp.inf)
        p = jax.nn.softmax(logits, axis=-1)
        outs.append(jnp.einsum('bhqk,bhkd->bhqd', p.astype(v.dtype), vb))
    return jnp.concatenate(outs, axis=2)


def _causal_depthwise_conv(x, w, b):
    y = lax.conv_general_dilated(
        x, w[:, None, :], window_strides=(1,), padding=[(CONV_WIDTH - 1, 0)],
        dimension_numbers=('NWC', 'WIO', 'NWC'), feature_group_count=x.shape[-1])
    return y + b


def _rg_lru(xc, w_a, b_a, w_x, b_x, lam):
    B, S, _ = xc.shape
    xb = xc.reshape(B, S, LRU_BLOCKS, LRU_BLOCK_DIM)
    r = jax.nn.sigmoid(jnp.einsum('bsni,nij->bsnj', xb, w_a).reshape(B, S, LRU_WIDTH) + b_a)
    i = jax.nn.sigmoid(jnp.einsum('bsni,nij->bsnj', xb, w_x).reshape(B, S, LRU_WIDTH) + b_x)
    log_a = -LRU_C * r.astype(jnp.float32) * jax.nn.softplus(-lam.astype(jnp.float32))
    a = jnp.exp(log_a)
    bterm = jnp.sqrt(-jnp.expm1(2.0 * log_a)) * (i * xc).astype(jnp.float32)

    def combine(lhs, rhs):
        a1, b1 = lhs
        a2, b2 = rhs
        return a1 * a2, a2 * b1 + b2

    _, h = lax.associative_scan(combine, (a, bterm), axis=1)
    return h.astype(xc.dtype)


def _hybrid_mixer(h, w_in, b_f, conv_w, conv_b, lru_w_a, lru_b_a, lru_w_x, lru_b_x,
                  lru_lambda, g_attn_out, g_lru_out, w_out):
    B, S, _ = h.shape
    proj = h @ w_in

    def heads(t):
        return t.reshape(B, S, ATTN_HEADS, ATTN_HEAD_DIM).transpose(0, 2, 1, 3)

    q = heads(proj[..., OFF_Q:OFF_K])
    k = heads(proj[..., OFF_K:OFF_V])
    v = heads(proj[..., OFF_V:OFF_F])
    f_logit = (proj[..., OFF_F:OFF_LX] + b_f).astype(jnp.float32)
    log_f = jax.nn.log_sigmoid(f_logit).transpose(0, 2, 1)
    attn = _forgetting_attention(q, k, v, log_f)
    attn = attn.transpose(0, 2, 1, 3).reshape(B, S, ATTN_WIDTH)

    lx = proj[..., OFF_LX:OFF_LY]
    ly = proj[..., OFF_LY:IN_COLS]
    xc = _causal_depthwise_conv(lx, conv_w, conv_b)
    lru = _rg_lru(xc, lru_w_a, lru_b_a, lru_w_x, lru_b_x, lru_lambda) * jax.nn.gelu(ly)

    merged = jnp.concatenate([_rmsnorm(attn, g_attn_out), _rmsnorm(lru, g_lru_out)], axis=-1)
    return merged @ w_out


def _peer(h, w_q, k1, k2, u, vt):
    B, S, D = h.shape
    q = (h @ w_q).reshape(B, S, PEER_HEADS, 2, PEER_HALF)
    s1 = jnp.einsum('bshd,hnd->bshn', q[..., 0, :], k1).astype(jnp.float32)
    s2 = jnp.einsum('bshd,hnd->bshn', q[..., 1, :], k2).astype(jnp.float32)
    v1, i1 = lax.top_k(s1, PEER_TOPK)
    v2, i2 = lax.top_k(s2, PEER_TOPK)
    cand = (v1[..., :, None] + v2[..., None, :]).reshape(B, S, PEER_HEADS, PEER_TOPK * PEER_TOPK)
    vs, ic = lax.top_k(cand, PEER_TOPK)
    e = (jnp.take_along_axis(i1, ic // PEER_TOPK, axis=-1) * PEER_N_KEYS
         + jnp.take_along_axis(i2, ic % PEER_TOPK, axis=-1))
    w = jax.nn.softmax(vs, axis=-1).astype(h.dtype)

    nb = (B * S) // PEER_TOKEN_BLOCK
    hb = h.reshape(nb, PEER_TOKEN_BLOCK, D)
    eb = e.reshape(nb, PEER_TOKEN_BLOCK, PEER_HEADS, PEER_TOPK)
    wb = w.reshape(nb, PEER_TOKEN_BLOCK, PEER_HEADS, PEER_TOPK)

    def expert_block(args):
        hx, ex, wx = args
        ug = jnp.take(u, ex, axis=0)
        act = jax.nn.gelu(jnp.einsum('thkd,td->thk', ug, hx)) * wx
        vg = jnp.take(vt, ex, axis=0)
        return jnp.einsum('thk,thkd->td', act, vg)

    y = lax.map(expert_block, (hb, eb, wb))
    return y.reshape(B, S, D)


def setup_inputs(seed: int = 0) -> dict:
    key = jax.random.key(seed)
    ks = jax.random.split(key, 26)
    f32 = jnp.float32
    L, D = DEPTH, D_MODEL

    def nrm(k, shape, scale):
        return jax.random.normal(k, shape, f32) * scale

    a0 = jax.random.uniform(ks[12], (L, LRU_WIDTH), f32, 0.9, 0.999) ** (1.0 / LRU_C)
    return {
        'x': nrm(ks[0], (BATCH, SEQ, D), 1.0),
        'c': nrm(ks[1], (BATCH, D), 1.0),
        'w_ada': nrm(ks[2], (L, D, N_ADA * D), 0.5 * D ** -0.5),
        'b_ada': nrm(ks[3], (L, N_ADA * D), 0.01),
        'g_mix': 1.0 + nrm(ks[4], (L, D), 0.05),
        'w_in': nrm(ks[5], (L, D, IN_COLS), D ** -0.5),
        'b_f': jax.random.uniform(ks[6], (L, ATTN_HEADS), f32, 1.0, 4.0),
        'conv_w': nrm(ks[7], (L, CONV_WIDTH, LRU_WIDTH), CONV_WIDTH ** -0.5),
        'conv_b': nrm(ks[8], (L, LRU_WIDTH), 0.01),
        'lru_w_a': nrm(ks[9], (L, LRU_BLOCKS, LRU_BLOCK_DIM, LRU_BLOCK_DIM), LRU_BLOCK_DIM ** -0.5),
        'lru_b_a': nrm(ks[10], (L, LRU_WIDTH), 0.01),
        'lru_w_x': nrm(ks[11], (L, LRU_BLOCKS, LRU_BLOCK_DIM, LRU_BLOCK_DIM), LRU_BLOCK_DIM ** -0.5),
        'lru_b_x': nrm(ks[13], (L, LRU_WIDTH), 0.01),
        'lru_lambda': jnp.log(a0) - jnp.log1p(-a0),
        'g_attn_out': 1.0 + nrm(ks[14], (L, ATTN_WIDTH), 0.05),
        'g_lru_out': 1.0 + nrm(ks[15], (L, LRU_WIDTH), 0.05),
        'w_out': nrm(ks[16], (L, MIX_WIDTH, D), MIX_WIDTH ** -0.5),
        'g_ffn': 1.0 + nrm(ks[17], (L, D), 0.05),
        'peer_w_q': nrm(ks[18], (L, D, PEER_HEADS * PEER_QUERY_DIM), D ** -0.5),
        'peer_k1': nrm(ks[19], (L, PEER_HEADS, PEER_N_KEYS, PEER_HALF), PEER_HALF ** -0.5),
        'peer_k2': nrm(ks[20], (L, PEER_HEADS, PEER_N_KEYS, PEER_HALF), PEER_HALF ** -0.5),
        'peer_u': nrm(ks[21], (L, PEER_N_EXPERTS, D), D ** -0.5),
        'peer_v': nrm(ks[22], (L, PEER_N_EXPERTS, D), 1.0),
        'g_final': 1.0 + nrm(ks[23], (D,), 0.05),
    }


def reference(x, c, w_ada, b_ada, g_mix, w_in, b_f, conv_w, conv_b, lru_w_a, lru_b_a,
              lru_w_x, lru_b_x, lru_lambda, g_attn_out, g_lru_out, w_out, g_ffn,
              peer_w_q, peer_k1, peer_k2, peer_u, peer_v, g_final):
    sc = jax.nn.silu(c)
    for l in range(DEPTH):
        mod = sc @ w_ada[l] + b_ada[l]
        sh1, sc1, ga1, sh2, sc2, ga2 = jnp.split(mod, N_ADA, axis=-1)
        h = _rmsnorm(x, g_mix[l]) * (1.0 + sc1[:, None, :]) + sh1[:, None, :]
        mix = _hybrid_mixer(h, w_in[l], b_f[l], conv_w[l], conv_b[l], lru_w_a[l], lru_b_a[l],
                            lru_w_x[l], lru_b_x[l], lru_lambda[l], g_attn_out[l],
                            g_lru_out[l], w_out[l])
        x = x + ga1[:, None, :] * mix
        h = _rmsnorm(x, g_ffn[l]) * (1.0 + sc2[:, None, :]) + sh2[:, None, :]
        x = x + ga2[:, None, :] * _peer(h, peer_w_q[l], peer_k1[l], peer_k2[l], peer_u[l], peer_v[l])
    return _rmsnorm(x, g_final)
```

```python
import functools

import jax
import jax.numpy as jnp
from jax import lax
from jax.experimental import pallas as pl
from jax.experimental.pallas import tpu as pltpu

F32 = jnp.float32
BF16 = jnp.bfloat16

ATTN_HEADS = 8
HEAD_DIM = 128
ATTN_WIDTH = ATTN_HEADS * HEAD_DIM
LRU_BLOCKS = 8
LRU_BLOCK_DIM = 128
LRU_WIDTH = LRU_BLOCKS * LRU_BLOCK_DIM
CONV_WIDTH = 4
LRU_C = 8.0
PEER_HEADS = 8
PEER_KEYS = 128
PEER_HALF = 128
PEER_TOPK = 16
N_ADA = 6
EPS = 1e-6
LANES = 128
SUBLANES = 8
NEG_BIG = -1e30
VMEM_LIMIT = 56 * 1024 * 1024

_NT = (((1,), (1,)), ((), ()))


def _params(sem, vmem=VMEM_LIMIT):
    return pltpu.CompilerParams(dimension_semantics=sem, vmem_limit_bytes=vmem)


def _gelu_tanh(x):
    return 0.5 * x * (1.0 + jnp.tanh(0.7978845608028654 * (x + 0.044715 * (x * x * x))))


def _softplus(z):
    return jnp.maximum(z, 0.0) + jnp.log1p(jnp.exp(-jnp.abs(z)))


def _rms(x):
    return x * lax.rsqrt(jnp.mean(x * x, axis=-1, keepdims=True) + EPS)


def _ada_kernel(c_ref, w_ref, b_ref, o_ref):
    c = c_ref[...]
    sc = c * jax.nn.sigmoid(c)
    o_ref[...] = jnp.dot(sc.astype(BF16), w_ref[...].astype(BF16),
                         preferred_element_type=F32) + b_ref[...]


def _ada(c, w, b, tn=1024):
    bsz, d = c.shape
    n = w.shape[1]
    return pl.pallas_call(
        _ada_kernel,
        grid=(n // tn,),
        in_specs=[pl.BlockSpec((bsz, d), lambda j: (0, 0)),
                  pl.BlockSpec((d, tn), lambda j: (0, j)),
                  pl.BlockSpec((1, tn), lambda j: (0, j))],
        out_specs=pl.BlockSpec((bsz, tn), lambda j: (0, j)),
        out_shape=jax.ShapeDtypeStruct((bsz, n), F32),
        compiler_params=_params(("arbitrary",)),
        name="ada",
    )(c, w, b.reshape(1, n))


def _modnorm_kernel(x_ref, g_ref, sc_ref, sh_ref, o_ref):
    y = _rms(x_ref[...]) * g_ref[...]
    o_ref[...] = (y * (1.0 + sc_ref[0]) + sh_ref[0]).astype(o_ref.dtype)


def _modnorm(x2, g, sc, sh, seq, tm=512):
    t, d = x2.shape
    nb = seq // tm
    return pl.pallas_call(
        _modnorm_kernel,
        grid=(t // tm,),
        in_specs=[pl.BlockSpec((tm, d), lambda i: (i, 0)),
                  pl.BlockSpec((1, d), lambda i: (0, 0)),
                  pl.BlockSpec((1, 1, d), lambda i: (i // nb, 0, 0)),
                  pl.BlockSpec((1, 1, d), lambda i: (i // nb, 0, 0))],
        out_specs=pl.BlockSpec((tm, d), lambda i: (i, 0)),
        out_shape=jax.ShapeDtypeStruct((t, d), BF16),
        compiler_params=_params(("arbitrary",)),
        name="modnorm",
    )(x2, g.reshape(1, d), sc, sh)


def _matmul_kernel(a_ref, w_ref, o_ref):
    o_ref[...] = jnp.dot(a_ref[...], w_ref[...],
                         preferred_element_type=F32).astype(o_ref.dtype)


def _matmul(a, w, out_dtype, tm=512, tn=1024, name="matmul"):
    t, k = a.shape
    n = w.shape[1]
    tn = min(tn, n)
    return pl.pallas_call(
        _matmul_kernel,
        grid=(n // tn, t // tm),
        in_specs=[pl.BlockSpec((tm, k), lambda j, i: (i, 0)),
                  pl.BlockSpec((k, tn), lambda j, i: (0, j))],
        out_specs=pl.BlockSpec((tm, tn), lambda j, i: (i, j)),
        out_shape=jax.ShapeDtypeStruct((t, n), out_dtype),
        compiler_params=_params(("arbitrary", "arbitrary")),
        name=name,
    )(a, w)


def _cum_kernel(fl_ref, bf_ref, col_ref, row_ref, carry_ref, *, ts):
    @pl.when(pl.program_id(1) == 0)
    def _():
        carry_ref[...] = jnp.zeros_like(carry_ref)

    lf = -_softplus(-(fl_ref[...] + bf_ref[...]))
    rows = lax.broadcasted_iota(jnp.int32, lf.shape, 0)
    s = 1
    while s < ts:
        lf = lf + jnp.where(rows >= s, pltpu.roll(lf, s, axis=0), 0.0)
        s *= 2
    cum = lf + carry_ref[...]
    carry_ref[...] = cum[ts - 1:ts, :]
    col_ref[...] = cum
    row_ref[0] = cum.T[0:ATTN_HEADS, :]


def _cum(fl, bf, bsz, seq, ts=512):
    t = fl.shape[0]
    ns = seq // ts
    return pl.pallas_call(
        functools.partial(_cum_kernel, ts=ts),
        grid=(bsz, ns),
        in_specs=[pl.BlockSpec((ts, LANES), lambda b, s: (b * ns + s, 0)),
                  pl.BlockSpec((1, LANES), lambda b, s: (0, 0))],
        out_specs=[pl.BlockSpec((ts, LANES), lambda b, s: (b * ns + s, 0)),
                   pl.BlockSpec((1, ATTN_HEADS, ts), lambda b, s: (b, 0, s))],
        out_shape=[jax.ShapeDtypeStruct((t, LANES), F32),
                   jax.ShapeDtypeStruct((bsz, ATTN_HEADS, seq), F32)],
        scratch_shapes=[pltpu.VMEM((1, LANES), F32)],
        compiler_params=_params(("arbitrary", "arbitrary")),
        name="cum",
    )(fl, bf)


def _attn_kernel(q_ref, k_ref, v_ref, cc_ref, cr_ref, o_ref, m_ref, l_ref, acc_ref, *, tq, nk):
    h = pl.program_id(1)
    qi = pl.program_id(2)
    q = q_ref[...]
    lane = lax.broadcasted_iota(jnp.int32, (tq, LANES), 1)
    cq = jnp.sum(jnp.where(lane == h, cc_ref[...], 0.0), axis=1, keepdims=True)
    m_ref[...] = jnp.full_like(m_ref, NEG_BIG)
    l_ref[...] = jnp.zeros_like(l_ref)
    acc_ref[...] = jnp.zeros_like(acc_ref)
    scale = HEAD_DIM ** -0.5

    def step(j, masked):
        ks = pl.multiple_of(j * tq, tq)
        k = k_ref[pl.ds(ks, tq), :]
        v = v_ref[pl.ds(ks, tq), :]
        s = lax.dot_general(q, k, _NT, preferred_element_type=F32) * scale
        ck = cr_ref[0, pl.ds(h * nk + j, 1), :]
        s = s + (cq - ck)
        if masked:
            row = lax.broadcasted_iota(jnp.int32, (tq, tq), 0)
            col = lax.broadcasted_iota(jnp.int32, (tq, tq), 1)
            s = jnp.where(col <= row, s, NEG_BIG)
        m_prev = m_ref[...]
        m_new = jnp.maximum(m_prev, jnp.max(s, axis=1, keepdims=True))
        alpha = jnp.exp(m_prev - m_new)
        p = jnp.exp(s - m_new)
        l_ref[...] = alpha * l_ref[...] + jnp.sum(p, axis=1, keepdims=True)
        acc_ref[...] = alpha * acc_ref[...] + jnp.dot(p.astype(BF16), v,
                                                      preferred_element_type=F32)
        m_ref[...] = m_new

    def body(j, carry):
        step(j, False)
        return carry

    lax.fori_loop(0, qi, body, 0)
    step(qi, True)
    o_ref[...] = acc_ref[...] / l_ref[...]


def _attn(qkv, cum_col, cum_row, bsz, seq, tq=512):
    t = qkv.shape[0]
    nq = seq // tq
    h = ATTN_HEADS
    cr = cum_row.reshape(bsz, h * nq, tq)
    return pl.pallas_call(
        functools.partial(_attn_kernel, tq=tq, nk=nq),
        grid=(bsz, h, nq),
        in_specs=[pl.BlockSpec((tq, HEAD_DIM), lambda b, hh, i: (b * nq + i, hh)),
                  pl.BlockSpec((seq, HEAD_DIM), lambda b, hh, i: (b, h + hh)),
                  pl.BlockSpec((seq, HEAD_DIM), lambda b, hh, i: (b, 2 * h + hh)),
                  pl.BlockSpec((tq, LANES), lambda b, hh, i: (b * nq + i, 0)),
                  pl.BlockSpec((1, h * nq, tq), lambda b, hh, i: (b, 0, 0))],
        out_specs=pl.BlockSpec((tq, HEAD_DIM), lambda b, hh, i: (b * nq + i, hh)),
        out_shape=jax.ShapeDtypeStruct((t, ATTN_WIDTH), F32),
        scratch_shapes=[pltpu.VMEM((tq, 1), F32), pltpu.VMEM((tq, 1), F32),
                        pltpu.VMEM((tq, HEAD_DIM), F32)],
        compiler_params=_params(("arbitrary", "arbitrary", "arbitrary")),
        name="attn",
    )(qkv, qkv, qkv, cum_col, cr)


def _lru_kernel(lx_ref, ly_ref, cw_ref, cb_ref, wa_ref, ba_ref, wx_ref, bx_ref, lam_ref,
                g_ref, o_ref, xbuf, hc, *, ts):
    @pl.when(pl.program_id(1) == 0)
    def _():
        xbuf[0:SUBLANES, :] = jnp.zeros((SUBLANES, LRU_WIDTH), F32)
        hc[...] = jnp.zeros_like(hc)

    xbuf[SUBLANES:, :] = lx_ref[...]
    xc = cb_ref[...]
    for k in range(CONV_WIDTH):
        off = SUBLANES - (CONV_WIDTH - 1) + k
        xc = xc + cw_ref[k:k + 1, :] * xbuf[off:off + ts, :]
    xbuf[0:SUBLANES, :] = xbuf[ts:ts + SUBLANES, :]

    xcb = xc.astype(BF16)
    ra, rx = [], []
    for n in range(LRU_BLOCKS):
        xs = xcb[:, n * LRU_BLOCK_DIM:(n + 1) * LRU_BLOCK_DIM]
        ra.append(jnp.dot(xs, wa_ref[n], preferred_element_type=F32))
        rx.append(jnp.dot(xs, wx_ref[n], preferred_element_type=F32))
    r = jax.nn.sigmoid(jnp.concatenate(ra, axis=1) + ba_ref[...])
    ig = jax.nn.sigmoid(jnp.concatenate(rx, axis=1) + bx_ref[...])
    log_a = (-LRU_C) * r * _softplus(-lam_ref[...])
    a = jnp.exp(log_a)
    b = jnp.sqrt(-jnp.tanh(log_a) * (a * a + 1.0)) * (ig * xc)

    rows = lax.broadcasted_iota(jnp.int32, a.shape, 0)
    s = 1
    while s < ts:
        keep = rows >= s
        b = jnp.where(keep, a * pltpu.roll(b, s, axis=0) + b, b)
        a = jnp.where(keep, a * pltpu.roll(a, s, axis=0), a)
        s *= 2
    hseq = b + a * hc[...]
    hc[...] = hseq[ts - 1:ts, :]

    lru = hseq * _gelu_tanh(ly_ref[...])
    o_ref[...] = (_rms(lru) * g_ref[...]).astype(o_ref.dtype)


def _lru(lxy, conv_w, conv_b, w_a, b_a, w_x, b_x, lam, g, bsz, seq, ts=256):
    t = lxy.shape[0]
    ns = seq // ts
    w = LRU_WIDTH
    vec = lambda: pl.BlockSpec((1, w), lambda b, s: (0, 0))
    blk = lambda: pl.BlockSpec((LRU_BLOCKS, LRU_BLOCK_DIM, LRU_BLOCK_DIM), lambda b, s: (0, 0, 0))
    return pl.pallas_call(
        functools.partial(_lru_kernel, ts=ts),
        grid=(bsz, ns),
        in_specs=[pl.BlockSpec((ts, w), lambda b, s: (b * ns + s, 0)),
                  pl.BlockSpec((ts, w), lambda b, s: (b * ns + s, 1)),
                  pl.BlockSpec((CONV_WIDTH, w), lambda b, s: (0, 0)),
                  vec(), blk(), vec(), blk(), vec(), vec(), vec()],
        out_specs=pl.BlockSpec((ts, w), lambda b, s: (b * ns + s, 0)),
        out_shape=jax.ShapeDtypeStruct((t, w), BF16),
        scratch_shapes=[pltpu.VMEM((ts + SUBLANES, w), F32), pltpu.VMEM((1, w), F32)],
        compiler_params=_params(("arbitrary", "arbitrary")),
        name="lru",
    )(lxy, lxy, conv_w, conv_b.reshape(1, w), w_a.astype(BF16), b_a.reshape(1, w),
      w_x.astype(BF16), b_x.reshape(1, w), lam.reshape(1, w), g.reshape(1, w))


def _outproj_kernel(at_ref, lr_ref, x_ref, ga_ref, woa_ref, wol_ref, g1_ref, sc_ref, sh_ref,
                    gf_ref, x1_ref, h2_ref):
    an = (_rms(at_ref[...]) * ga_ref[...]).astype(BF16)
    mix = jnp.dot(an, woa_ref[...], preferred_element_type=F32)
    mix = mix + jnp.dot(lr_ref[...], wol_ref[...], preferred_element_type=F32)
    x1 = x_ref[...] + g1_ref[0] * mix
    x1_ref[...] = x1
    h2 = (_rms(x1) * gf_ref[...]) * (1.0 + sc_ref[0]) + sh_ref[0]
    h2_ref[...] = h2.astype(h2_ref.dtype)


def _outproj(attn, lru_n, x2, g_attn, w_oa, w_ol, ga1, sc2, sh2, g_ffn, seq, tm=256):
    t, d = x2.shape
    nb = seq // tm
    aw = attn.shape[1]
    lw = lru_n.shape[1]
    mod = lambda: pl.BlockSpec((1, 1, d), lambda i: (i // nb, 0, 0))
    return pl.pallas_call(
        _outproj_kernel,
        grid=(t // tm,),
        in_specs=[pl.BlockSpec((tm, aw), lambda i: (i, 0)),
                  pl.BlockSpec((tm, lw), lambda i: (i, 0)),
                  pl.BlockSpec((tm, d), lambda i: (i, 0)),
                  pl.BlockSpec((1, aw), lambda i: (0, 0)),
                  pl.BlockSpec((aw, d), lambda i: (0, 0)),
                  pl.BlockSpec((lw, d), lambda i: (0, 0)),
                  mod(), mod(), mod(),
                  pl.BlockSpec((1, d), lambda i: (0, 0))],
        out_specs=[pl.BlockSpec((tm, d), lambda i: (i, 0)),
                   pl.BlockSpec((tm, d), lambda i: (i, 0))],
        out_shape=[jax.ShapeDtypeStruct((t, d), F32), jax.ShapeDtypeStruct((t, d), BF16)],
        compiler_params=_params(("arbitrary",)),
        name="outproj",
    )(attn, lru_n, x2, g_attn.reshape(1, aw), w_oa, w_ol, ga1, sc2, sh2, g_ffn.reshape(1, d))


def _sort16_pairs():
    pairs = []

    def merge(lo, hi, r):
        step = r * 2
        if step < hi - lo:
            merge(lo, hi, step)
            merge(lo + r, hi, step)
            for i in range(lo + r, hi - r, step):
                pairs.append((i, i + r))
        else:
            pairs.append((lo, lo + r))

    def sort(lo, hi):
        if hi - lo >= 1:
            mid = lo + (hi - lo) // 2
            sort(lo, mid)
            sort(mid + 1, hi)
            merge(lo, hi, 1)

    sort(0, PEER_TOPK - 1)
    return pairs


_SORT16 = _sort16_pairs()


def _cex(v, i, j):
    hi, lo = jnp.maximum(v[i], v[j]), jnp.minimum(v[i], v[j])
    v[i], v[j] = hi, lo


def _sort16(v):
    v = list(v)
    for i, j in _SORT16:
        _cex(v, i, j)
    return v


def _bitonic_merge16(v):
    v = list(v)
    d = PEER_TOPK // 2
    while d >= 1:
        for i in range(PEER_TOPK):
            if i & d == 0:
                _cex(v, i, i + d)
        d //= 2
    return v


def _top16_of_union(a, b):
    return [jnp.maximum(a[i], b[PEER_TOPK - 1 - i]) for i in range(PEER_TOPK)]


def _top16_sorted(s):
    v = _sort16([s[i * SUBLANES:(i + 1) * SUBLANES, :] for i in range(PEER_TOPK)])
    for shift in (4, 2, 1):
        other = [pltpu.roll(x, shift, axis=0) for x in v]
        v = _bitonic_merge16(_top16_of_union(v, other))
    return v


def _threshold_stats(v1, v2):
    k = PEER_TOPK
    cell = lambda i, j: v1[i] + v2[j]
    row0 = [cell(0, j) for j in range(k)]
    col0 = [cell(i, 0) for i in range(1, k)]
    rest = ([(1, j) for j in range(1, 8)] + [(i, 1) for i in range(2, 8)]
            + [(2, j) for j in range(2, 5)] + [(i, 2) for i in range(3, 5)] + [(3, 3)])
    rest = [cell(i, j) for i, j in rest]
    ab = [row0[0]] + [jnp.maximum(row0[i], col0[k - 1 - i]) for i in range(1, k)]
    ab = _bitonic_merge16(ab)
    c1 = _sort16(rest[:k])
    t = _bitonic_merge16(_top16_of_union(ab, c1))
    y = list(rest[k:])
    _cex(y, 0, 1), _cex(y, 1, 2), _cex(y, 0, 1)
    thr = jnp.minimum(jnp.minimum(t[12], jnp.maximum(t[13], y[2])),
                      jnp.minimum(jnp.maximum(t[14], y[1]), jnp.maximum(t[15], y[0])))
    top = row0[0]
    z = jnp.zeros_like(top)
    for c in row0 + col0 + rest:
        z = z + jnp.where(c >= thr, jnp.exp(c - top), 0.0)
    return thr, z


def _topk_kernel(qp_ref, k1_ref, k2_ref, s1_ref, a1_ref, s2_ref, b2_ref, thr_ref):
    def head(h, carry):
        q = qp_ref[h]
        s1 = lax.dot_general(k1_ref[h], q[:, :PEER_HALF], _NT, preferred_element_type=F32)
        s2 = lax.dot_general(k2_ref[h], q[:, PEER_HALF:], _NT, preferred_element_type=F32)
        v1 = _top16_sorted(s1)
        v2 = _top16_sorted(s2)
        thr, z = _threshold_stats(v1, v2)
        s1_ref[0, h] = s1
        s2_ref[0, h] = s2
        a1_ref[0, h] = jnp.exp(s1 - v1[0][0:1, :]) / z[0:1, :]
        b2_ref[0, h] = jnp.exp(s2 - v2[0][0:1, :])
        thr_ref[0, h] = thr
        return carry

    lax.fori_loop(0, PEER_HEADS, head, 0)


def _topk(qp3, k1, k2):
    nh, t, qd = qp3.shape
    nc = t // LANES
    big = lambda: pl.BlockSpec((1, nh, PEER_KEYS, LANES), lambda i: (i, 0, 0, 0))
    keys = lambda: pl.BlockSpec((nh, PEER_KEYS, PEER_HALF), lambda i: (0, 0, 0))
    sds = jax.ShapeDtypeStruct((nc, nh, PEER_KEYS, LANES), F32)
    return pl.pallas_call(
        _topk_kernel,
        grid=(nc,),
        in_specs=[pl.BlockSpec((nh, LANES, qd), lambda i: (0, i, 0)), keys(), keys()],
        out_specs=[big(), big(), big(), big(),
                   pl.BlockSpec((1, nh, SUBLANES, LANES), lambda i: (i, 0, 0, 0))],
        out_shape=[sds, sds, sds, sds, jax.ShapeDtypeStruct((nc, nh, SUBLANES, LANES), F32)],
        compiler_params=_params(("arbitrary",)),
        name="topk",
    )(qp3, k1, k2)


PEER_TE = SUBLANES * PEER_KEYS


def _peer_kernel(h_ref, u_ref, vt_ref, s1_ref, a1_ref, s2_ref, b2_ref, thr_ref, y_ref,
                 acc, st, act, *, nc):
    j = pl.program_id(1)

    @pl.when(j == 0)
    def _():
        acc[...] = jnp.zeros_like(acc)

    scores = lax.dot_general(u_ref[...], h_ref[...], _NT, preferred_element_type=F32)
    for c in range(nc):
        st[c] = scores[:, c * LANES:(c + 1) * LANES]
    row0 = pl.multiple_of(j * SUBLANES, SUBLANES)

    def chunk(c, carry):
        for r in range(SUBLANES):
            w = jnp.zeros((PEER_KEYS, LANES), F32)
            for h in range(PEER_HEADS):
                s1g = s1_ref[c, h, pl.ds(row0, SUBLANES), :]
                a1g = a1_ref[c, h, pl.ds(row0, SUBLANES), :]
                thr = jnp.tile(thr_ref[c, h], (PEER_KEYS // SUBLANES, 1))
                ssum = s1g[r:r + 1, :] + s2_ref[c, h]
                w = w + jnp.where(ssum >= thr, b2_ref[c, h], 0.0) * a1g[r:r + 1, :]
            rows = slice(r * PEER_KEYS, (r + 1) * PEER_KEYS)
            act[c, rows, :] = (_gelu_tanh(st[c, rows, :]) * w).astype(BF16)
        return carry

    lax.fori_loop(0, nc, chunk, 0)
    act_all = jnp.concatenate([act[c] for c in range(nc)], axis=1)
    acc[...] += jnp.dot(vt_ref[...], act_all, preferred_element_type=F32)

    @pl.when(j == pl.num_programs(1) - 1)
    def _():
        y_ref[...] = acc[...].T


def _peer(h2, u_b, vt_t, s1, a1, s2, b2, thr, tm=512):
    t, d = h2.shape
    e = u_b.shape[0]
    te = PEER_TE
    nh = s1.shape[1]
    nc = tm // LANES
    big = lambda: pl.BlockSpec((nc, nh, PEER_KEYS, LANES), lambda i, j: (i, 0, 0, 0))
    return pl.pallas_call(
        functools.partial(_peer_kernel, nc=nc),
        grid=(t // tm, e // te),
        in_specs=[pl.BlockSpec((tm, d), lambda i, j: (i, 0)),
                  pl.BlockSpec((te, d), lambda i, j: (j, 0)),
                  pl.BlockSpec((d, te), lambda i, j: (0, j)),
                  big(), big(), big(), big(),
                  pl.BlockSpec((nc, nh, SUBLANES, LANES), lambda i, j: (i, 0, 0, 0))],
        out_specs=pl.BlockSpec((tm, d), lambda i, j: (i, 0)),
        out_shape=jax.ShapeDtypeStruct((t, d), F32),
        scratch_shapes=[pltpu.VMEM((d, tm), F32), pltpu.VMEM((nc, te, LANES), F32),
                        pltpu.VMEM((nc, te, LANES), BF16)],
        compiler_params=_params(("arbitrary", "arbitrary")),
        name="peer",
    )(h2, u_b, vt_t, s1, a1, s2, b2, thr)


def _final_kernel(x1_ref, y_ref, g2_ref, gf_ref, o_ref):
    o_ref[...] = _rms(x1_ref[...] + g2_ref[0] * y_ref[...]) * gf_ref[...]


def _final(x1, y, ga2, g_final, seq, tm=512):
    t, d = x1.shape
    nb = seq // tm
    return pl.pallas_call(
        _final_kernel,
        grid=(t // tm,),
        in_specs=[pl.BlockSpec((tm, d), lambda i: (i, 0)),
                  pl.BlockSpec((tm, d), lambda i: (i, 0)),
                  pl.BlockSpec((1, 1, d), lambda i: (i // nb, 0, 0)),
                  pl.BlockSpec((1, d), lambda i: (0, 0))],
        out_specs=pl.BlockSpec((tm, d), lambda i: (i, 0)),
        out_shape=jax.ShapeDtypeStruct((t, d), F32),
        compiler_params=_params(("arbitrary",)),
        name="final",
    )(x1, y, ga2, g_final.reshape(1, d))


def kernel(x, c, w_ada, b_ada, g_mix, w_in, b_f, conv_w, conv_b, lru_w_a, lru_b_a, lru_w_x,
           lru_b_x, lru_lambda, g_attn_out, g_lru_out, w_out, g_ffn, peer_w_q, peer_k1,
           peer_k2, peer_u, peer_v, g_final):
    bsz, seq, d = x.shape
    depth = w_ada.shape[0]
    assert depth == 1, "the final residual + norm is fused for a single layer"
    x2 = x.reshape(bsz * seq, d)
    off_f = 3 * ATTN_WIDTH
    off_l = off_f + ATTN_HEADS

    for l in range(depth):
        mod = _ada(c, w_ada[l], b_ada[l])
        sh1, sc1, ga1, sh2, sc2, ga2 = [m.reshape(bsz, 1, d) for m in jnp.split(mod, N_ADA, axis=-1)]

        h = _modnorm(x2, g_mix[l], sc1, sh1, seq)
        w_l = w_in[l]
        qkv = _matmul(h, w_l[:, :off_f].astype(BF16), BF16, name="proj_qkv")
        w_f = jnp.pad(w_l[:, off_f:off_l], ((0, 0), (0, LANES - ATTN_HEADS))).astype(BF16)
        fl = _matmul(h, w_f, F32, name="proj_f")
        lxy = _matmul(h, w_l[:, off_l:].astype(BF16), F32, name="proj_lru")

        bf = jnp.pad(b_f[l], (0, LANES - ATTN_HEADS)).reshape(1, LANES)
        cum_col, cum_row = _cum(fl, bf, bsz, seq)
        attn = _attn(qkv, cum_col, cum_row, bsz, seq)
        lru_n = _lru(lxy, conv_w[l], conv_b[l], lru_w_a[l], lru_b_a[l], lru_w_x[l], lru_b_x[l],
                     lru_lambda[l], g_lru_out[l], bsz, seq)

        w_o = w_out[l].astype(BF16)
        x2, h2 = _outproj(attn, lru_n, x2, g_attn_out[l], w_o[:ATTN_WIDTH], w_o[ATTN_WIDTH:],
                          ga1, sc2, sh2, g_ffn[l], seq)

        qp = _matmul(h2, peer_w_q[l].astype(BF16), BF16, name="peer_q")
        qp3 = qp.reshape(bsz * seq, PEER_HEADS, 2 * PEER_HALF).transpose(1, 0, 2)
        s1, a1, s2, b2, thr = _topk(qp3, peer_k1[l].astype(BF16), peer_k2[l].astype(BF16))
        y = _peer(h2, peer_u[l].astype(BF16), peer_v[l].T.astype(BF16), s1, a1, s2, b2, thr)
        x1 = x2
        x2 = None
    return _final(x1, y, ga2, g_final, seq).reshape(bsz, seq, d)
```

```python
import functools

import jax
import jax.numpy as jnp
from jax import lax
from jax.experimental import pallas as pl
from jax.experimental.pallas import tpu as pltpu

F32 = jnp.float32
BF16 = jnp.bfloat16

ATTN_HEADS = 8
HEAD_DIM = 128
ATTN_WIDTH = ATTN_HEADS * HEAD_DIM
LRU_BLOCKS = 8
LRU_BLOCK_DIM = 128
LRU_WIDTH = LRU_BLOCKS * LRU_BLOCK_DIM
CONV_WIDTH = 4
LRU_C = 8.0
PEER_HEADS = 8
PEER_KEYS = 128
PEER_HALF = 128
PEER_TOPK = 16
N_ADA = 6
EPS = 1e-6
LANES = 128
SUBLANES = 8
NEG_BIG = -1e30
VMEM_LIMIT = 56 * 1024 * 1024

_NT = (((1,), (1,)), ((), ()))


def _params(sem, vmem=VMEM_LIMIT):
    return pltpu.CompilerParams(dimension_semantics=sem, vmem_limit_bytes=vmem)


def _gelu_tanh(x):
    return 0.5 * x * (1.0 + jnp.tanh(0.7978845608028654 * (x + 0.044715 * (x * x * x))))


def _softplus(z):
    return jnp.maximum(z, 0.0) + jnp.log1p(jnp.exp(-jnp.abs(z)))


def _rms(x):
    return x * lax.rsqrt(jnp.mean(x * x, axis=-1, keepdims=True) + EPS)


def _ada_kernel(c_ref, w_ref, b_ref, o_ref):
    c = c_ref[...]
    sc = c * jax.nn.sigmoid(c)
    o_ref[...] = jnp.dot(sc.astype(BF16), w_ref[...].astype(BF16),
                         preferred_element_type=F32) + b_ref[...]


def _ada(c, w, b, tn=1024):
    bsz, d = c.shape
    n = w.shape[1]
    return pl.pallas_call(
        _ada_kernel,
        grid=(n // tn,),
        in_specs=[pl.BlockSpec((bsz, d), lambda j: (0, 0)),
                  pl.BlockSpec((d, tn), lambda j: (0, j)),
                  pl.BlockSpec((1, tn), lambda j: (0, j))],
        out_specs=pl.BlockSpec((bsz, tn), lambda j: (0, j)),
        out_shape=jax.ShapeDtypeStruct((bsz, n), F32),
        compiler_params=_params(("arbitrary",)),
        name="ada",
    )(c, w, b.reshape(1, n))


def _modnorm_kernel(x_ref, g_ref, sc_ref, sh_ref, o_ref):
    y = _rms(x_ref[...]) * g_ref[...]
    o_ref[...] = (y * (1.0 + sc_ref[0]) + sh_ref[0]).astype(o_ref.dtype)


def _modnorm(x2, g, sc, sh, seq, tm=512):
    t, d = x2.shape
    nb = seq // tm
    return pl.pallas_call(
        _modnorm_kernel,
        grid=(t // tm,),
        in_specs=[pl.BlockSpec((tm, d), lambda i: (i, 0)),
                  pl.BlockSpec((1, d), lambda i: (0, 0)),
                  pl.BlockSpec((1, 1, d), lambda i: (i // nb, 0, 0)),
                  pl.BlockSpec((1, 1, d), lambda i: (i // nb, 0, 0))],
        out_specs=pl.BlockSpec((tm, d), lambda i: (i, 0)),
        out_shape=jax.ShapeDtypeStruct((t, d), BF16),
        compiler_params=_params(("arbitrary",)),
        name="modnorm",
    )(x2, g.reshape(1, d), sc, sh)


def _matmul_kernel(a_ref, w_ref, o_ref):
    o_ref[...] = jnp.dot(a_ref[...], w_ref[...],
                         preferred_element_type=F32).astype(o_ref.dtype)


def _matmul(a, w, out_dtype, tm=512, tn=1024, name="matmul"):
    t, k = a.shape
    n = w.shape[1]
    tn = min(tn, n)
    return pl.pallas_call(
        _matmul_kernel,
        grid=(n // tn, t // tm),
        in_specs=[pl.BlockSpec((tm, k), lambda j, i: (i, 0)),
                  pl.BlockSpec((k, tn), lambda j, i: (0, j))],
        out_specs=pl.BlockSpec((tm, tn), lambda j, i: (i, j)),
        out_shape=jax.ShapeDtypeStruct((t, n), out_dtype),
        compiler_params=_params(("arbitrary", "arbitrary")),
        name=name,
    )(a, w)


def _matmul_t_kernel(a_ref, wt_ref, o_ref):
    o_ref[0] = lax.dot_general(wt_ref[...], a_ref[...], _NT,
                               preferred_element_type=F32).astype(o_ref.dtype)


def _matmul_t(a, wt, out_dtype, tm=512, tn=1024, name="matmul_t"):
    t, k = a.shape
    n = wt.shape[0]
    tn = min(tn, n)
    return pl.pallas_call(
        _matmul_t_kernel,
        grid=(n // tn, t // tm),
        in_specs=[pl.BlockSpec((tm, k), lambda j, i: (i, 0)),
                  pl.BlockSpec((tn, k), lambda j, i: (j, 0))],
        out_specs=pl.BlockSpec((1, tn, tm), lambda j, i: (i, j, 0)),
        out_shape=jax.ShapeDtypeStruct((t // tm, n, tm), out_dtype),
        compiler_params=_params(("arbitrary", "arbitrary")),
        name=name,
    )(a, wt)


def _cum_kernel(fl_ref, bf_ref, rep_ref, row_ref, carry_ref, *, ts):
    @pl.when(pl.program_id(1) == 0)
    def _():
        carry_ref[...] = jnp.zeros_like(carry_ref)

    lf = -_softplus(-(fl_ref[...] + bf_ref[...]))
    rows = lax.broadcasted_iota(jnp.int32, lf.shape, 0)
    s = 1
    while s < ts:
        lf = lf + jnp.where(rows >= s, pltpu.roll(lf, s, axis=0), 0.0)
        s *= 2
    cum = lf + carry_ref[...]
    carry_ref[...] = cum[ts - 1:ts, :]
    for h in range(ATTN_HEADS):
        rep_ref[0, h] = jnp.broadcast_to(cum[:, h:h + 1], (ts, LANES))
    row_ref[0] = cum.T[0:ATTN_HEADS, :]


def _cum(fl, bf, bsz, seq, ts=512):
    ns = seq // ts
    return pl.pallas_call(
        functools.partial(_cum_kernel, ts=ts),
        grid=(bsz, ns),
        in_specs=[pl.BlockSpec((ts, LANES), lambda b, s: (b * ns + s, 0)),
                  pl.BlockSpec((1, LANES), lambda b, s: (0, 0))],
        out_specs=[pl.BlockSpec((1, ATTN_HEADS, ts, LANES), lambda b, s: (b, 0, s, 0)),
                   pl.BlockSpec((1, ATTN_HEADS, ts), lambda b, s: (b, 0, s))],
        out_shape=[jax.ShapeDtypeStruct((bsz, ATTN_HEADS, seq, LANES), F32),
                   jax.ShapeDtypeStruct((bsz, ATTN_HEADS, seq), F32)],
        scratch_shapes=[pltpu.VMEM((1, LANES), F32)],
        compiler_params=_params(("arbitrary", "arbitrary")),
        name="cum",
    )(fl, bf)


def _attn_kernel(q_ref, k_ref, vt_ref, cq_ref, ck_ref, o_ref, m_ref, l_ref, acc_ref, *, tq):
    qi = pl.program_id(2)
    q = q_ref[...]
    cq = cq_ref[0, 0, 0]
    m_ref[...] = jnp.full_like(m_ref, NEG_BIG)
    l_ref[...] = jnp.zeros_like(l_ref)
    acc_ref[...] = jnp.zeros_like(acc_ref)
    scale = HEAD_DIM ** -0.5
    reps = tq // LANES

    def step(j, masked):
        ks = pl.multiple_of(j * tq, tq)
        k = k_ref[pl.ds(ks, tq), :]
        ck = ck_ref[0, 0, pl.ds(ks, tq), :]
        s = lax.dot_general(k, q, _NT, preferred_element_type=F32) * scale
        s = s + (cq - jnp.concatenate([ck] * reps, axis=1))
        if masked:
            key = lax.broadcasted_iota(jnp.int32, (tq, tq), 0)
            qry = lax.broadcasted_iota(jnp.int32, (tq, tq), 1)
            s = jnp.where(key <= qry, s, NEG_BIG)
        m_prev = m_ref[...]
        m_new = jnp.maximum(m_prev, jnp.max(s, axis=0, keepdims=True))
        alpha = jnp.exp(m_prev - m_new)
        p = jnp.exp(s - m_new)
        l_ref[...] = alpha * l_ref[...] + jnp.sum(p, axis=0, keepdims=True)
        acc_ref[...] = alpha * acc_ref[...] + jnp.dot(vt_ref[j], p.astype(BF16),
                                                      preferred_element_type=F32)
        m_ref[...] = m_new

    def body(j, carry):
        step(j, False)
        return carry

    lax.fori_loop(0, qi, body, 0)
    step(qi, True)
    o_ref[...] = (acc_ref[...] / l_ref[...]).T


ATTN_TQ = 512


def _attn(qk, vt, cum_rep, cum_row, bsz, seq, tq=ATTN_TQ):
    t = qk.shape[0]
    nq = seq // tq
    h = ATTN_HEADS
    cq = cum_row.reshape(bsz, h, nq, 1, tq)
    return pl.pallas_call(
        functools.partial(_attn_kernel, tq=tq),
        grid=(bsz, h, nq),
        in_specs=[pl.BlockSpec((tq, HEAD_DIM), lambda b, hh, i: (b * nq + i, hh)),
                  pl.BlockSpec((seq, HEAD_DIM), lambda b, hh, i: (b, h + hh)),
                  pl.BlockSpec((nq, HEAD_DIM, tq), lambda b, hh, i: (b, hh, 0)),
                  pl.BlockSpec((1, 1, 1, 1, tq), lambda b, hh, i: (b, hh, i, 0, 0)),
                  pl.BlockSpec((1, 1, seq, LANES), lambda b, hh, i: (b, hh, 0, 0))],
        out_specs=pl.BlockSpec((tq, HEAD_DIM), lambda b, hh, i: (b * nq + i, hh)),
        out_shape=jax.ShapeDtypeStruct((t, ATTN_WIDTH), F32),
        scratch_shapes=[pltpu.VMEM((1, tq), F32), pltpu.VMEM((1, tq), F32),
                        pltpu.VMEM((HEAD_DIM, tq), F32)],
        compiler_params=_params(("arbitrary", "arbitrary", "arbitrary")),
        name="attn",
    )(qk, qk, vt, cq, cum_rep)


def _lru_kernel(lx_ref, ly_ref, cw_ref, cb_ref, wa_ref, ba_ref, wx_ref, bx_ref, lam_ref,
                g_ref, o_ref, xbuf, hc, *, ts):
    @pl.when(pl.program_id(1) == 0)
    def _():
        xbuf[0:SUBLANES, :] = jnp.zeros((SUBLANES, LRU_WIDTH), F32)
        hc[...] = jnp.zeros_like(hc)

    xbuf[SUBLANES:, :] = lx_ref[...]
    xc = cb_ref[...]
    for k in range(CONV_WIDTH):
        off = SUBLANES - (CONV_WIDTH - 1) + k
        xc = xc + cw_ref[k:k + 1, :] * xbuf[off:off + ts, :]
    xbuf[0:SUBLANES, :] = xbuf[ts:ts + SUBLANES, :]

    xcb = xc.astype(BF16)
    ra, rx = [], []
    for n in range(LRU_BLOCKS):
        xs = xcb[:, n * LRU_BLOCK_DIM:(n + 1) * LRU_BLOCK_DIM]
        ra.append(jnp.dot(xs, wa_ref[n], preferred_element_type=F32))
        rx.append(jnp.dot(xs, wx_ref[n], preferred_element_type=F32))
    r = jax.nn.sigmoid(jnp.concatenate(ra, axis=1) + ba_ref[...])
    ig = jax.nn.sigmoid(jnp.concatenate(rx, axis=1) + bx_ref[...])
    log_a = (-LRU_C) * r * _softplus(-lam_ref[...])
    a = jnp.exp(log_a)
    b = jnp.sqrt(-jnp.tanh(log_a) * (a * a + 1.0)) * (ig * xc)

    rows = lax.broadcasted_iota(jnp.int32, a.shape, 0)
    s = 1
    while s < ts:
        keep = rows >= s
        b = jnp.where(keep, a * pltpu.roll(b, s, axis=0) + b, b)
        a = jnp.where(keep, a * pltpu.roll(a, s, axis=0), a)
        s *= 2
    hseq = b + a * hc[...]
    hc[...] = hseq[ts - 1:ts, :]

    lru = hseq * _gelu_tanh(ly_ref[...])
    o_ref[...] = (_rms(lru) * g_ref[...]).astype(o_ref.dtype)


def _lru(lxy, conv_w, conv_b, w_a, b_a, w_x, b_x, lam, g, bsz, seq, ts=256):
    t = lxy.shape[0]
    ns = seq // ts
    w = LRU_WIDTH
    vec = lambda: pl.BlockSpec((1, w), lambda b, s: (0, 0))
    blk = lambda: pl.BlockSpec((LRU_BLOCKS, LRU_BLOCK_DIM, LRU_BLOCK_DIM), lambda b, s: (0, 0, 0))
    return pl.pallas_call(
        functools.partial(_lru_kernel, ts=ts),
        grid=(bsz, ns),
        in_specs=[pl.BlockSpec((ts, w), lambda b, s: (b * ns + s, 0)),
                  pl.BlockSpec((ts, w), lambda b, s: (b * ns + s, 1)),
                  pl.BlockSpec((CONV_WIDTH, w), lambda b, s: (0, 0)),
                  vec(), blk(), vec(), blk(), vec(), vec(), vec()],
        out_specs=pl.BlockSpec((ts, w), lambda b, s: (b * ns + s, 0)),
        out_shape=jax.ShapeDtypeStruct((t, w), BF16),
        scratch_shapes=[pltpu.VMEM((ts + SUBLANES, w), F32), pltpu.VMEM((1, w), F32)],
        compiler_params=_params(("arbitrary", "arbitrary")),
        name="lru",
    )(lxy, lxy, conv_w, conv_b.reshape(1, w), w_a.astype(BF16), b_a.reshape(1, w),
      w_x.astype(BF16), b_x.reshape(1, w), lam.reshape(1, w), g.reshape(1, w))


def _outproj_kernel(at_ref, lr_ref, x_ref, ga_ref, woa_ref, wol_ref, g1_ref, sc_ref, sh_ref,
                    gf_ref, x1_ref, h2_ref):
    an = (_rms(at_ref[...]) * ga_ref[...]).astype(BF16)
    mix = jnp.dot(an, woa_ref[...], preferred_element_type=F32)
    mix = mix + jnp.dot(lr_ref[...], wol_ref[...], preferred_element_type=F32)
    x1 = x_ref[...] + g1_ref[0] * mix
    x1_ref[...] = x1
    h2 = (_rms(x1) * gf_ref[...]) * (1.0 + sc_ref[0]) + sh_ref[0]
    h2_ref[...] = h2.astype(h2_ref.dtype)


def _outproj(attn, lru_n, x2, g_attn, w_oa, w_ol, ga1, sc2, sh2, g_ffn, seq, tm=256):
    t, d = x2.shape
    nb = seq // tm
    aw = attn.shape[1]
    lw = lru_n.shape[1]
    mod = lambda: pl.BlockSpec((1, 1, d), lambda i: (i // nb, 0, 0))
    return pl.pallas_call(
        _outproj_kernel,
        grid=(t // tm,),
        in_specs=[pl.BlockSpec((tm, aw), lambda i: (i, 0)),
                  pl.BlockSpec((tm, lw), lambda i: (i, 0)),
                  pl.BlockSpec((tm, d), lambda i: (i, 0)),
                  pl.BlockSpec((1, aw), lambda i: (0, 0)),
                  pl.BlockSpec((aw, d), lambda i: (0, 0)),
                  pl.BlockSpec((lw, d), lambda i: (0, 0)),
                  mod(), mod(), mod(),
                  pl.BlockSpec((1, d), lambda i: (0, 0))],
        out_specs=[pl.BlockSpec((tm, d), lambda i: (i, 0)),
                   pl.BlockSpec((tm, d), lambda i: (i, 0))],
        out_shape=[jax.ShapeDtypeStruct((t, d), F32), jax.ShapeDtypeStruct((t, d), BF16)],
        compiler_params=_params(("arbitrary",)),
        name="outproj",
    )(attn, lru_n, x2, g_attn.reshape(1, aw), w_oa, w_ol, ga1, sc2, sh2, g_ffn.reshape(1, d))


def _sort16_pairs():
    pairs = []

    def merge(lo, hi, r):
        step = r * 2
        if step < hi - lo:
            merge(lo, hi, step)
            merge(lo + r, hi, step)
            for i in range(lo + r, hi - r, step):
                pairs.append((i, i + r))
        else:
            pairs.append((lo, lo + r))

    def sort(lo, hi):
        if hi - lo >= 1:
            mid = lo + (hi - lo) // 2
            sort(lo, mid)
            sort(mid + 1, hi)
            merge(lo, hi, 1)

    sort(0, PEER_TOPK - 1)
    return pairs


_SORT16 = _sort16_pairs()


def _cex(v, i, j):
    hi, lo = jnp.maximum(v[i], v[j]), jnp.minimum(v[i], v[j])
    v[i], v[j] = hi, lo


def _sort16(v):
    v = list(v)
    for i, j in _SORT16:
        _cex(v, i, j)
    return v


def _bitonic_merge16(v):
    v = list(v)
    d = PEER_TOPK // 2
    while d >= 1:
        for i in range(PEER_TOPK):
            if i & d == 0:
                _cex(v, i, i + d)
        d //= 2
    return v


def _top16_of_union(a, b):
    return [jnp.maximum(a[i], b[PEER_TOPK - 1 - i]) for i in range(PEER_TOPK)]


def _top16_sorted(s):
    v = _sort16([s[i * SUBLANES:(i + 1) * SUBLANES, :] for i in range(PEER_TOPK)])
    for shift in (4, 2, 1):
        other = [pltpu.roll(x, shift, axis=0) for x in v]
        v = _bitonic_merge16(_top16_of_union(v, other))
    return v


def _threshold_stats(v1, v2):
    k = PEER_TOPK
    cell = lambda i, j: v1[i] + v2[j]
    row0 = [cell(0, j) for j in range(k)]
    col0 = [cell(i, 0) for i in range(1, k)]
    rest = ([(1, j) for j in range(1, 8)] + [(i, 1) for i in range(2, 8)]
            + [(2, j) for j in range(2, 5)] + [(i, 2) for i in range(3, 5)] + [(3, 3)])
    rest = [cell(i, j) for i, j in rest]
    ab = [row0[0]] + [jnp.maximum(row0[i], col0[k - 1 - i]) for i in range(1, k)]
    ab = _bitonic_merge16(ab)
    c1 = _sort16(rest[:k])
    t = _bitonic_merge16(_top16_of_union(ab, c1))
    y = list(rest[k:])
    _cex(y, 0, 1), _cex(y, 1, 2), _cex(y, 0, 1)
    thr = jnp.minimum(jnp.minimum(t[12], jnp.maximum(t[13], y[2])),
                      jnp.minimum(jnp.maximum(t[14], y[1]), jnp.maximum(t[15], y[0])))
    top = row0[0]
    z = jnp.zeros_like(top)
    for c in row0 + col0 + rest:
        z = z + jnp.where(c >= thr, jnp.exp(c - top), 0.0)
    return thr, z


def _topk_kernel(qp_ref, k1_ref, k2_ref, n1_ref, a1_ref, r2_ref, b2_ref):
    reps = PEER_KEYS // SUBLANES

    def head(h, carry):
        q = qp_ref[h]
        s1 = lax.dot_general(k1_ref[h], q[:, :PEER_HALF], _NT, preferred_element_type=F32)
        s2 = lax.dot_general(k2_ref[h], q[:, PEER_HALF:], _NT, preferred_element_type=F32)
        v1 = _top16_sorted(s1)
        v2 = _top16_sorted(s2)
        thr, z = _threshold_stats(v1, v2)
        thr_t = jnp.tile(thr, (reps, 1))
        n1 = jnp.zeros_like(s1)
        r2 = jnp.zeros_like(s2)
        for v in v2:
            vt = jnp.tile(v, (reps, 1))
            n1 = n1 + jnp.where(s1 + vt >= thr_t, 1.0, 0.0)
            r2 = r2 + jnp.where(vt > s2, 1.0, 0.0)
        n1_ref[0, h] = n1
        r2_ref[0, h] = r2.astype(BF16)
        a1_ref[0, h] = jnp.exp(s1 - v1[0][0:1, :]) / z[0:1, :]
        b2_ref[0, h] = jnp.exp(s2 - v2[0][0:1, :]).astype(BF16)
        return carry

    lax.fori_loop(0, PEER_HEADS, head, 0)


def _topk(qp3, k1, k2):
    nh, t, qd = qp3.shape
    nc = t // LANES
    big = lambda: pl.BlockSpec((1, nh, PEER_KEYS, LANES), lambda i: (i, 0, 0, 0))
    keys = lambda: pl.BlockSpec((nh, PEER_KEYS, PEER_HALF), lambda i: (0, 0, 0))
    shape = (nc, nh, PEER_KEYS, LANES)
    return pl.pallas_call(
        _topk_kernel,
        grid=(nc,),
        in_specs=[pl.BlockSpec((nh, LANES, qd), lambda i: (0, i, 0)), keys(), keys()],
        out_specs=[big(), big(), big(), big()],
        out_shape=[jax.ShapeDtypeStruct(shape, F32), jax.ShapeDtypeStruct(shape, F32),
                   jax.ShapeDtypeStruct(shape, BF16), jax.ShapeDtypeStruct(shape, BF16)],
        compiler_params=_params(("arbitrary",)),
        name="topk",
    )(qp3, k1, k2)


PEER_ROWS = SUBLANES // 2
PEER_TE = PEER_ROWS * PEER_KEYS


def _peer_kernel(h_ref, u_ref, vt_ref, n1p_ref, a1p_ref, n1c_ref, a1c_ref, r2c_ref, b2c_ref, y_ref,
                 acc, st0, st1, act0, act1, r2s, b2s, *, nc, ne, n):
    g = pl.program_id(0)
    pair_c = jnp.maximum(g - 1, 0)
    slot_a = lax.rem(g // ne, 2)
    slot_c = lax.rem(pair_c // ne, 2)
    te = PEER_TE

    @pl.when(g == 0)
    def _():
        for buf in (st0, st1, act0, act1):
            buf[...] = jnp.zeros_like(buf)

    @pl.when(jnp.logical_and(lax.rem(g, ne) == 0, g < n))
    def _():
        for c in range(nc):
            for h in range(PEER_HEADS):
                r2s[slot_a, c, h] = r2c_ref[c, h]
                b2s[slot_a, c, h] = b2c_ref[c, h]

    @pl.when(lax.rem(pair_c, ne) == 0)
    def _():
        acc[...] = jnp.zeros_like(acc)

    tm = nc * LANES
    d = acc.shape[0]

    n_pieces = 16

    def score(tile, st):
        st[...] = lax.dot_general(u_ref[tile * te:(tile + 1) * te, :], h_ref[...], _NT,
                                  preferred_element_type=F32)

    def value(tile, act):
        def piece(q):
            ds = slice(q * d // n_pieces, (q + 1) * d // n_pieces)
            def run():
                acc[ds, :] += jnp.dot(vt_ref[ds, tile * te:(tile + 1) * te], act[...],
                                      preferred_element_type=F32)
            return run
        return [piece(q) for q in range(n_pieces)]

    def weight(tile, refs, slot, st, act):
        n1_ref, a1_ref = refs
        def block(c, r):
            cs = slice(c * LANES, (c + 1) * LANES)
            row = tile * PEER_ROWS + r
            rows = slice(r * PEER_KEYS, (r + 1) * PEER_KEYS)
            bcast = lambda x: jnp.broadcast_to(x.astype(BF16), (PEER_KEYS, LANES))
            def run():
                w = jnp.zeros((PEER_KEYS, LANES), BF16)
                for h in range(PEER_HEADS):
                    n1 = bcast(n1_ref[c, h, row:row + 1, :])
                    a1 = bcast(a1_ref[c, h, row:row + 1, :])
                    w = w + jnp.where(r2s[slot, c, h] < n1, b2s[slot, c, h], jnp.zeros((), BF16)) * a1
                act[rows, cs] = _gelu_tanh(st[rows, cs]).astype(BF16) * w
            return run
        return [block(c, r) for c in range(nc) for r in range(PEER_ROWS)]

    def interleave(pieces, blocks):
        per = len(blocks) // len(pieces)
        for i, p in enumerate(pieces):
            p()
            for b in blocks[i * per:(i + 1) * per]:
                b()

    prev = (n1p_ref, a1p_ref)
    cur = (n1c_ref, a1c_ref)

    @pl.when(g >= 0)
    def _():
        score(0, st0)

    @pl.when(g >= -1)
    def _():
        interleave(value(0, act0), weight(1, prev, slot_c, st1, act1))

    @pl.when(g >= -2)
    def _():
        score(1, st1)

    @pl.when(g >= -3)
    def _():
        interleave(value(1, act1), weight(0, cur, slot_a, st0, act0))

    @pl.when(jnp.logical_and(g >= 1, lax.rem(pair_c, ne) == ne - 1))
    def _():
        y_ref[...] = acc[...].T


def _peer(h2, u_b, vt_t, n1, a1, r2, b2, tm=512):
    t, d = h2.shape
    e = u_b.shape[0]
    tp = 2 * PEER_TE
    ne = e // tp
    n = (t // tm) * ne
    nh = n1.shape[1]
    nc = tm // LANES
    pair_a = lambda g: jnp.minimum(g, n - 1)
    pair_c = lambda g: jnp.clip(g - 1, 0, n - 1)
    once = pl.Buffered(1)

    rows = lambda pair: pl.BlockSpec((nc, nh, SUBLANES, LANES),
                                     lambda g: (pair(g) // ne, 0, pair(g) % ne, 0))
    full = lambda: pl.BlockSpec((nc, nh, PEER_KEYS, LANES), lambda g: (pair_a(g) // ne, 0, 0, 0),
                                pipeline_mode=once)
    packed = pltpu.VMEM((2, nc, nh, PEER_KEYS, LANES), BF16)
    return pl.pallas_call(
        functools.partial(_peer_kernel, nc=nc, ne=ne, n=n),
        grid=(n + 1,),
        in_specs=[pl.BlockSpec((tm, d), lambda g: (pair_a(g) // ne, 0)),
                  pl.BlockSpec((tp, d), lambda g: (pair_a(g) % ne, 0)),
                  pl.BlockSpec((d, tp), lambda g: (0, pair_c(g) % ne)),
                  rows(pair_c), rows(pair_c), rows(pair_a), rows(pair_a), full(), full()],
        out_specs=pl.BlockSpec((tm, d), lambda g: (pair_c(g) // ne, 0)),
        out_shape=jax.ShapeDtypeStruct((t, d), F32),
        scratch_shapes=[pltpu.VMEM((d, tm), F32),
                        pltpu.VMEM((PEER_TE, tm), F32), pltpu.VMEM((PEER_TE, tm), F32),
                        pltpu.VMEM((PEER_TE, tm), BF16), pltpu.VMEM((PEER_TE, tm), BF16),
                        packed, packed],
        compiler_params=_params(("arbitrary",)),
        name="peer",
    )(h2, u_b, vt_t, n1, a1, n1, a1, r2, b2)


def _final_kernel(x1_ref, y_ref, g2_ref, gf_ref, o_ref):
    o_ref[...] = _rms(x1_ref[...] + g2_ref[0] * y_ref[...]) * gf_ref[...]


def _final(x1, y, ga2, g_final, seq, tm=512):
    t, d = x1.shape
    nb = seq // tm
    return pl.pallas_call(
        _final_kernel,
        grid=(t // tm,),
        in_specs=[pl.BlockSpec((tm, d), lambda i: (i, 0)),
                  pl.BlockSpec((tm, d), lambda i: (i, 0)),
                  pl.BlockSpec((1, 1, d), lambda i: (i // nb, 0, 0)),
                  pl.BlockSpec((1, d), lambda i: (0, 0))],
        out_specs=pl.BlockSpec((tm, d), lambda i: (i, 0)),
        out_shape=jax.ShapeDtypeStruct((t, d), F32),
        compiler_params=_params(("arbitrary",)),
        name="final",
    )(x1, y, ga2, g_final.reshape(1, d))


def kernel(x, c, w_ada, b_ada, g_mix, w_in, b_f, conv_w, conv_b, lru_w_a, lru_b_a, lru_w_x,
           lru_b_x, lru_lambda, g_attn_out, g_lru_out, w_out, g_ffn, peer_w_q, peer_k1,
           peer_k2, peer_u, peer_v, g_final):
    bsz, seq, d = x.shape
    depth = w_ada.shape[0]
    assert depth == 1, "the final residual + norm is fused for a single layer"
    x2 = x.reshape(bsz * seq, d)
    off_f = 3 * ATTN_WIDTH
    off_l = off_f + ATTN_HEADS

    for l in range(depth):
        mod = _ada(c, w_ada[l], b_ada[l])
        sh1, sc1, ga1, sh2, sc2, ga2 = [m.reshape(bsz, 1, d) for m in jnp.split(mod, N_ADA, axis=-1)]

        h = _modnorm(x2, g_mix[l], sc1, sh1, seq)
        w_l = w_in[l]
        qk = _matmul(h, w_l[:, :2 * ATTN_WIDTH].astype(BF16), BF16, name="proj_qk")
        vt = _matmul_t(h, w_l[:, 2 * ATTN_WIDTH:off_f].T.astype(BF16), BF16, tm=ATTN_TQ,
                       name="proj_vt")
        w_f = jnp.pad(w_l[:, off_f:off_l], ((0, 0), (0, LANES - ATTN_HEADS))).astype(BF16)
        fl = _matmul(h, w_f, F32, name="proj_f")
        lxy = _matmul(h, w_l[:, off_l:].astype(BF16), F32, name="proj_lru")

        bf = jnp.pad(b_f[l], (0, LANES - ATTN_HEADS)).reshape(1, LANES)
        cum_rep, cum_row = _cum(fl, bf, bsz, seq)
        attn = _attn(qk, vt, cum_rep, cum_row, bsz, seq)
        lru_n = _lru(lxy, conv_w[l], conv_b[l], lru_w_a[l], lru_b_a[l], lru_w_x[l], lru_b_x[l],
                     lru_lambda[l], g_lru_out[l], bsz, seq)

        w_o = w_out[l].astype(BF16)
        x2, h2 = _outproj(attn, lru_n, x2, g_attn_out[l], w_o[:ATTN_WIDTH], w_o[ATTN_WIDTH:],
                          ga1, sc2, sh2, g_ffn[l], seq)

        qp = _matmul(h2, peer_w_q[l].astype(BF16), BF16, name="peer_q")
        qp3 = qp.reshape(bsz * seq, PEER_HEADS, 2 * PEER_HALF).transpose(1, 0, 2)
        n1, a1, r2, b2 = _topk(qp3, peer_k1[l].astype(BF16), peer_k2[l].astype(BF16))
        y = _peer(h2, peer_u[l].astype(BF16), peer_v[l].T.astype(BF16), n1, a1, r2, b2)
        x1 = x2
        x2 = None
    return _final(x1, y, ga2, g_final, seq).reshape(bsz, seq, d)
```

```python
import functools

import jax
import jax.numpy as jnp
from jax import lax
from jax.experimental import pallas as pl
from jax.experimental.pallas import tpu as pltpu

F32 = jnp.float32
BF16 = jnp.bfloat16

ATTN_HEADS = 8
HEAD_DIM = 128
ATTN_WIDTH = ATTN_HEADS * HEAD_DIM
LRU_BLOCKS = 8
LRU_BLOCK_DIM = 128
LRU_WIDTH = LRU_BLOCKS * LRU_BLOCK_DIM
CONV_WIDTH = 4
LRU_C = 8.0
PEER_HEADS = 8
PEER_KEYS = 128
PEER_HALF = 128
PEER_TOPK = 16
N_ADA = 6
EPS = 1e-6
LANES = 128
SUBLANES = 8
NEG_BIG = -1e30
VMEM_LIMIT = 56 * 1024 * 1024

_NT = (((1,), (1,)), ((), ()))


def _params(sem, vmem=VMEM_LIMIT):
    return pltpu.CompilerParams(dimension_semantics=sem, vmem_limit_bytes=vmem)


def _gelu_tanh(x):
    return 0.5 * x * (1.0 + jnp.tanh(0.7978845608028654 * (x + 0.044715 * (x * x * x))))


def _softplus(z):
    return jnp.maximum(z, 0.0) + jnp.log1p(jnp.exp(-jnp.abs(z)))


def _rms(x):
    return x * lax.rsqrt(jnp.mean(x * x, axis=-1, keepdims=True) + EPS)


def _ada_kernel(c_ref, w_ref, b_ref, o_ref):
    c = c_ref[...]
    sc = c * jax.nn.sigmoid(c)
    o_ref[...] = jnp.dot(sc.astype(BF16), w_ref[...].astype(BF16),
                         preferred_element_type=F32) + b_ref[...]


def _ada(c, w, b, tn=1024):
    bsz, d = c.shape
    n = w.shape[1]
    return pl.pallas_call(
        _ada_kernel,
        grid=(n // tn,),
        in_specs=[pl.BlockSpec((bsz, d), lambda j: (0, 0)),
                  pl.BlockSpec((d, tn), lambda j: (0, j)),
                  pl.BlockSpec((1, tn), lambda j: (0, j))],
        out_specs=pl.BlockSpec((bsz, tn), lambda j: (0, j)),
        out_shape=jax.ShapeDtypeStruct((bsz, n), F32),
        compiler_params=_params(("arbitrary",)),
        name="ada",
    )(c, w, b.reshape(1, n))


def _modnorm_kernel(x_ref, g_ref, sc_ref, sh_ref, o_ref):
    y = _rms(x_ref[...]) * g_ref[...]
    o_ref[...] = (y * (1.0 + sc_ref[0]) + sh_ref[0]).astype(o_ref.dtype)


def _modnorm(x2, g, sc, sh, seq, tm=512):
    t, d = x2.shape
    nb = seq // tm
    return pl.pallas_call(
        _modnorm_kernel,
        grid=(t // tm,),
        in_specs=[pl.BlockSpec((tm, d), lambda i: (i, 0)),
                  pl.BlockSpec((1, d), lambda i: (0, 0)),
                  pl.BlockSpec((1, 1, d), lambda i: (i // nb, 0, 0)),
                  pl.BlockSpec((1, 1, d), lambda i: (i // nb, 0, 0))],
        out_specs=pl.BlockSpec((tm, d), lambda i: (i, 0)),
        out_shape=jax.ShapeDtypeStruct((t, d), BF16),
        compiler_params=_params(("arbitrary",)),
        name="modnorm",
    )(x2, g.reshape(1, d), sc, sh)


def _matmul_kernel(a_ref, w_ref, o_ref):
    o_ref[...] = jnp.dot(a_ref[...], w_ref[...],
                         preferred_element_type=F32).astype(o_ref.dtype)


def _matmul(a, w, out_dtype, tm=512, tn=1024, name="matmul"):
    t, k = a.shape
    n = w.shape[1]
    tn = min(tn, n)
    return pl.pallas_call(
        _matmul_kernel,
        grid=(n // tn, t // tm),
        in_specs=[pl.BlockSpec((tm, k), lambda j, i: (i, 0)),
                  pl.BlockSpec((k, tn), lambda j, i: (0, j))],
        out_specs=pl.BlockSpec((tm, tn), lambda j, i: (i, j)),
        out_shape=jax.ShapeDtypeStruct((t, n), out_dtype),
        compiler_params=_params(("arbitrary", "arbitrary")),
        name=name,
    )(a, w)


def _matmul_heads_kernel(a_ref, w_ref, o_ref):
    res = jnp.dot(a_ref[...], w_ref[...], preferred_element_type=F32).astype(o_ref.dtype)
    width = o_ref.shape[2]
    for h in range(o_ref.shape[0]):
        o_ref[h] = res[:, h * width:(h + 1) * width]


def _matmul_heads(a, w, nh, out_dtype, tm=512, name="matmul_heads"):
    t, k = a.shape
    n = w.shape[1]
    return pl.pallas_call(
        _matmul_heads_kernel,
        grid=(t // tm,),
        in_specs=[pl.BlockSpec((tm, k), lambda i: (i, 0)),
                  pl.BlockSpec((k, n), lambda i: (0, 0))],
        out_specs=pl.BlockSpec((nh, tm, n // nh), lambda i: (0, i, 0)),
        out_shape=jax.ShapeDtypeStruct((nh, t, n // nh), out_dtype),
        compiler_params=_params(("arbitrary",)),
        name=name,
    )(a, w)


def _matmul_t_kernel(a_ref, wt_ref, o_ref):
    o_ref[0] = lax.dot_general(wt_ref[...], a_ref[...], _NT,
                               preferred_element_type=F32).astype(o_ref.dtype)


def _matmul_t(a, wt, out_dtype, tm=512, tn=1024, name="matmul_t"):
    t, k = a.shape
    n = wt.shape[0]
    tn = min(tn, n)
    return pl.pallas_call(
        _matmul_t_kernel,
        grid=(n // tn, t // tm),
        in_specs=[pl.BlockSpec((tm, k), lambda j, i: (i, 0)),
                  pl.BlockSpec((tn, k), lambda j, i: (j, 0))],
        out_specs=pl.BlockSpec((1, tn, tm), lambda j, i: (i, j, 0)),
        out_shape=jax.ShapeDtypeStruct((t // tm, n, tm), out_dtype),
        compiler_params=_params(("arbitrary", "arbitrary")),
        name=name,
    )(a, wt)


def _cum_kernel(fl_ref, bf_ref, rep_ref, row_ref, carry_ref, *, ts):
    @pl.when(pl.program_id(1) == 0)
    def _():
        carry_ref[...] = jnp.zeros_like(carry_ref)

    lf = -_softplus(-(fl_ref[...] + bf_ref[...]))
    rows = lax.broadcasted_iota(jnp.int32, lf.shape, 0)
    s = 1
    while s < ts:
        lf = lf + jnp.where(rows >= s, pltpu.roll(lf, s, axis=0), 0.0)
        s *= 2
    cum = lf + carry_ref[...]
    carry_ref[...] = cum[ts - 1:ts, :]
    for h in range(ATTN_HEADS):
        rep_ref[0, h] = jnp.broadcast_to(cum[:, h:h + 1], (ts, LANES))
    row_ref[0] = cum.T[0:ATTN_HEADS, :]


def _cum(fl, bf, bsz, seq, ts=512):
    ns = seq // ts
    return pl.pallas_call(
        functools.partial(_cum_kernel, ts=ts),
        grid=(bsz, ns),
        in_specs=[pl.BlockSpec((ts, LANES), lambda b, s: (b * ns + s, 0)),
                  pl.BlockSpec((1, LANES), lambda b, s: (0, 0))],
        out_specs=[pl.BlockSpec((1, ATTN_HEADS, ts, LANES), lambda b, s: (b, 0, s, 0)),
                   pl.BlockSpec((1, ATTN_HEADS, ts), lambda b, s: (b, 0, s))],
        out_shape=[jax.ShapeDtypeStruct((bsz, ATTN_HEADS, seq, LANES), F32),
                   jax.ShapeDtypeStruct((bsz, ATTN_HEADS, seq), F32)],
        scratch_shapes=[pltpu.VMEM((1, LANES), F32)],
        compiler_params=_params(("arbitrary", "arbitrary")),
        name="cum",
    )(fl, bf)


def _attn_kernel(q_ref, k_ref, vt_ref, cq_ref, ck_ref, o_ref, m_ref, l_ref, acc_ref, *, tq):
    qi = pl.program_id(2)
    q = q_ref[...]
    cq = cq_ref[0, 0, 0]
    m_ref[...] = jnp.full_like(m_ref, NEG_BIG)
    l_ref[...] = jnp.zeros_like(l_ref)
    acc_ref[...] = jnp.zeros_like(acc_ref)
    scale = HEAD_DIM ** -0.5
    reps = tq // LANES

    def step(j, masked):
        ks = pl.multiple_of(j * tq, tq)
        k = k_ref[pl.ds(ks, tq), :]
        ck = ck_ref[0, 0, pl.ds(ks, tq), :]
        s = lax.dot_general(k, q, _NT, preferred_element_type=F32) * scale
        s = s + (cq - jnp.concatenate([ck] * reps, axis=1))
        if masked:
            key = lax.broadcasted_iota(jnp.int32, (tq, tq), 0)
            qry = lax.broadcasted_iota(jnp.int32, (tq, tq), 1)
            s = jnp.where(key <= qry, s, NEG_BIG)
        m_prev = m_ref[...]
        m_new = jnp.maximum(m_prev, jnp.max(s, axis=0, keepdims=True))
        alpha = jnp.exp(m_prev - m_new)
        p = jnp.exp(s - m_new)
        l_ref[...] = alpha * l_ref[...] + jnp.sum(p, axis=0, keepdims=True)
        acc_ref[...] = alpha * acc_ref[...] + jnp.dot(vt_ref[j], p.astype(BF16),
                                                      preferred_element_type=F32)
        m_ref[...] = m_new

    def body(j, carry):
        step(j, False)
        return carry

    lax.fori_loop(0, qi, body, 0)
    step(qi, True)
    o_ref[...] = (acc_ref[...] / l_ref[...]).T


ATTN_TQ = 512


def _attn(qk, vt, cum_rep, cum_row, bsz, seq, tq=ATTN_TQ):
    t = qk.shape[0]
    nq = seq // tq
    h = ATTN_HEADS
    cq = cum_row.reshape(bsz, h, nq, 1, tq)
    return pl.pallas_call(
        functools.partial(_attn_kernel, tq=tq),
        grid=(bsz, h, nq),
        in_specs=[pl.BlockSpec((tq, HEAD_DIM), lambda b, hh, i: (b * nq + i, hh)),
                  pl.BlockSpec((seq, HEAD_DIM), lambda b, hh, i: (b, h + hh)),
                  pl.BlockSpec((nq, HEAD_DIM, tq), lambda b, hh, i: (b, hh, 0)),
                  pl.BlockSpec((1, 1, 1, 1, tq), lambda b, hh, i: (b, hh, i, 0, 0)),
                  pl.BlockSpec((1, 1, seq, LANES), lambda b, hh, i: (b, hh, 0, 0))],
        out_specs=pl.BlockSpec((tq, HEAD_DIM), lambda b, hh, i: (b * nq + i, hh)),
        out_shape=jax.ShapeDtypeStruct((t, ATTN_WIDTH), F32),
        scratch_shapes=[pltpu.VMEM((1, tq), F32), pltpu.VMEM((1, tq), F32),
                        pltpu.VMEM((HEAD_DIM, tq), F32)],
        compiler_params=_params(("arbitrary", "arbitrary", "arbitrary")),
        name="attn",
    )(qk, qk, vt, cq, cum_rep)


def _lru_kernel(lx_ref, ly_ref, cw_ref, cb_ref, wa_ref, ba_ref, wx_ref, bx_ref, lam_ref,
                g_ref, o_ref, xbuf, hc, *, ts):
    @pl.when(pl.program_id(1) == 0)
    def _():
        xbuf[0:SUBLANES, :] = jnp.zeros((SUBLANES, LRU_WIDTH), F32)
        hc[...] = jnp.zeros_like(hc)

    xbuf[SUBLANES:, :] = lx_ref[...]
    xc = cb_ref[...]
    for k in range(CONV_WIDTH):
        off = SUBLANES - (CONV_WIDTH - 1) + k
        xc = xc + cw_ref[k:k + 1, :] * xbuf[off:off + ts, :]
    xbuf[0:SUBLANES, :] = xbuf[ts:ts + SUBLANES, :]

    xcb = xc.astype(BF16)
    ra, rx = [], []
    for n in range(LRU_BLOCKS):
        xs = xcb[:, n * LRU_BLOCK_DIM:(n + 1) * LRU_BLOCK_DIM]
        ra.append(jnp.dot(xs, wa_ref[n], preferred_element_type=F32))
        rx.append(jnp.dot(xs, wx_ref[n], preferred_element_type=F32))
    r = jax.nn.sigmoid(jnp.concatenate(ra, axis=1) + ba_ref[...])
    ig = jax.nn.sigmoid(jnp.concatenate(rx, axis=1) + bx_ref[...])
    log_a = (-LRU_C) * r * _softplus(-lam_ref[...])
    a = jnp.exp(log_a)
    b = jnp.sqrt(-jnp.tanh(log_a) * (a * a + 1.0)) * (ig * xc)

    rows = lax.broadcasted_iota(jnp.int32, a.shape, 0)
    s = 1
    while s < ts:
        keep = rows >= s
        b = jnp.where(keep, a * pltpu.roll(b, s, axis=0) + b, b)
        a = jnp.where(keep, a * pltpu.roll(a, s, axis=0), a)
        s *= 2
    hseq = b + a * hc[...]
    hc[...] = hseq[ts - 1:ts, :]

    lru = hseq * _gelu_tanh(ly_ref[...])
    o_ref[...] = (_rms(lru) * g_ref[...]).astype(o_ref.dtype)


def _lru(lxy, conv_w, conv_b, w_a, b_a, w_x, b_x, lam, g, bsz, seq, ts=256):
    t = lxy.shape[0]
    ns = seq // ts
    w = LRU_WIDTH
    vec = lambda: pl.BlockSpec((1, w), lambda b, s: (0, 0))
    blk = lambda: pl.BlockSpec((LRU_BLOCKS, LRU_BLOCK_DIM, LRU_BLOCK_DIM), lambda b, s: (0, 0, 0))
    return pl.pallas_call(
        functools.partial(_lru_kernel, ts=ts),
        grid=(bsz, ns),
        in_specs=[pl.BlockSpec((ts, w), lambda b, s: (b * ns + s, 0)),
                  pl.BlockSpec((ts, w), lambda b, s: (b * ns + s, 1)),
                  pl.BlockSpec((CONV_WIDTH, w), lambda b, s: (0, 0)),
                  vec(), blk(), vec(), blk(), vec(), vec(), vec()],
        out_specs=pl.BlockSpec((ts, w), lambda b, s: (b * ns + s, 0)),
        out_shape=jax.ShapeDtypeStruct((t, w), BF16),
        scratch_shapes=[pltpu.VMEM((ts + SUBLANES, w), F32), pltpu.VMEM((1, w), F32)],
        compiler_params=_params(("arbitrary", "arbitrary")),
        name="lru",
    )(lxy, lxy, conv_w, conv_b.reshape(1, w), w_a.astype(BF16), b_a.reshape(1, w),
      w_x.astype(BF16), b_x.reshape(1, w), lam.reshape(1, w), g.reshape(1, w))


def _outproj_kernel(at_ref, lr_ref, x_ref, ga_ref, woa_ref, wol_ref, g1_ref, sc_ref, sh_ref,
                    gf_ref, x1_ref, h2_ref, h2t_ref):
    an = (_rms(at_ref[...]) * ga_ref[...]).astype(BF16)
    mix = jnp.dot(an, woa_ref[...], preferred_element_type=F32)
    mix = mix + jnp.dot(lr_ref[...], wol_ref[...], preferred_element_type=F32)
    x1 = x_ref[...] + g1_ref[0] * mix
    x1_ref[...] = x1
    h2 = (_rms(x1) * gf_ref[...]) * (1.0 + sc_ref[0]) + sh_ref[0]
    h2_ref[...] = h2.astype(h2_ref.dtype)
    h2t_ref[...] = h2.T.astype(h2t_ref.dtype)


def _outproj(attn, lru_n, x2, g_attn, w_oa, w_ol, ga1, sc2, sh2, g_ffn, seq, tm=256):
    t, d = x2.shape
    nb = seq // tm
    aw = attn.shape[1]
    lw = lru_n.shape[1]
    mod = lambda: pl.BlockSpec((1, 1, d), lambda i: (i // nb, 0, 0))
    return pl.pallas_call(
        _outproj_kernel,
        grid=(t // tm,),
        in_specs=[pl.BlockSpec((tm, aw), lambda i: (i, 0)),
                  pl.BlockSpec((tm, lw), lambda i: (i, 0)),
                  pl.BlockSpec((tm, d), lambda i: (i, 0)),
                  pl.BlockSpec((1, aw), lambda i: (0, 0)),
                  pl.BlockSpec((aw, d), lambda i: (0, 0)),
                  pl.BlockSpec((lw, d), lambda i: (0, 0)),
                  mod(), mod(), mod(),
                  pl.BlockSpec((1, d), lambda i: (0, 0))],
        out_specs=[pl.BlockSpec((tm, d), lambda i: (i, 0)),
                   pl.BlockSpec((tm, d), lambda i: (i, 0)),
                   pl.BlockSpec((d, tm), lambda i: (0, i))],
        out_shape=[jax.ShapeDtypeStruct((t, d), F32), jax.ShapeDtypeStruct((t, d), BF16),
                   jax.ShapeDtypeStruct((d, t), BF16)],
        compiler_params=_params(("arbitrary",)),
        name="outproj",
    )(attn, lru_n, x2, g_attn.reshape(1, aw), w_oa, w_ol, ga1, sc2, sh2, g_ffn.reshape(1, d))


def _sort16_pairs():
    pairs = []

    def merge(lo, hi, r):
        step = r * 2
        if step < hi - lo:
            merge(lo, hi, step)
            merge(lo + r, hi, step)
            for i in range(lo + r, hi - r, step):
                pairs.append((i, i + r))
        else:
            pairs.append((lo, lo + r))

    def sort(lo, hi):
        if hi - lo >= 1:
            mid = lo + (hi - lo) // 2
            sort(lo, mid)
            sort(mid + 1, hi)
            merge(lo, hi, 1)

    sort(0, PEER_TOPK - 1)
    return pairs


_SORT16 = _sort16_pairs()


def _cex(v, i, j):
    hi, lo = jnp.maximum(v[i], v[j]), jnp.minimum(v[i], v[j])
    v[i], v[j] = hi, lo


def _sort16(v):
    v = list(v)
    for i, j in _SORT16:
        _cex(v, i, j)
    return v


def _bitonic_merge16(v):
    v = list(v)
    d = PEER_TOPK // 2
    while d >= 1:
        for i in range(PEER_TOPK):
            if i & d == 0:
                _cex(v, i, i + d)
        d //= 2
    return v


def _top16_of_union(a, b):
    return [jnp.maximum(a[i], b[PEER_TOPK - 1 - i]) for i in range(PEER_TOPK)]


def _top16_sorted(s):
    v = _sort16([s[i * SUBLANES:(i + 1) * SUBLANES, :] for i in range(PEER_TOPK)])
    for shift in (4, 2, 1):
        other = [pltpu.roll(x, shift, axis=0) for x in v]
        v = _bitonic_merge16(_top16_of_union(v, other))
    return v


def _threshold_stats(v1, v2):
    k = PEER_TOPK
    cell = lambda i, j: v1[i] + v2[j]
    row0 = [cell(0, j) for j in range(k)]
    col0 = [cell(i, 0) for i in range(1, k)]
    rest = ([(1, j) for j in range(1, 8)] + [(i, 1) for i in range(2, 8)]
            + [(2, j) for j in range(2, 5)] + [(i, 2) for i in range(3, 5)] + [(3, 3)])
    rest = [cell(i, j) for i, j in rest]
    ab = [row0[0]] + [jnp.maximum(row0[i], col0[k - 1 - i]) for i in range(1, k)]
    ab = _bitonic_merge16(ab)
    c1 = _sort16(rest[:k])
    t = _bitonic_merge16(_top16_of_union(ab, c1))
    y = list(rest[k:])
    _cex(y, 0, 1), _cex(y, 1, 2), _cex(y, 0, 1)
    thr = jnp.minimum(jnp.minimum(t[12], jnp.maximum(t[13], y[2])),
                      jnp.minimum(jnp.maximum(t[14], y[1]), jnp.maximum(t[15], y[0])))
    top = row0[0]
    z = jnp.zeros_like(top)
    for c in row0 + col0 + rest:
        z = z + jnp.where(c >= thr, jnp.exp(c - top), 0.0)
    return thr, z


def _topk_kernel(qp_ref, k1_ref, k2_ref, n1_ref, a1_ref, r2_ref, b2_ref):
    reps = PEER_KEYS // SUBLANES

    def head(h):
        q = qp_ref[h]
        s1 = lax.dot_general(k1_ref[h], q[:, :PEER_HALF], _NT, preferred_element_type=F32)
        s2 = lax.dot_general(k2_ref[h], q[:, PEER_HALF:], _NT, preferred_element_type=F32)
        v1 = _top16_sorted(s1)
        v2 = _top16_sorted(s2)
        thr, z = _threshold_stats(v1, v2)
        thr_t = jnp.tile(thr, (reps, 1))
        n1 = jnp.zeros_like(s1)
        r2 = jnp.zeros_like(s2)
        for v in v2:
            vt = jnp.tile(v, (reps, 1))
            n1 = n1 + jnp.where(s1 + vt >= thr_t, 1.0, 0.0)
            r2 = r2 + jnp.where(vt > s2, 1.0, 0.0)
        n1_ref[0, h] = n1
        r2_ref[0, h] = r2
        a1_ref[0, h] = jnp.exp(s1 - v1[0][0:1, :]) / z[0:1, :]
        b2_ref[0, h] = jnp.exp(s2 - v2[0][0:1, :])

    def two_heads(i, carry):
        head(2 * i)
        head(2 * i + 1)
        return carry

    lax.fori_loop(0, PEER_HEADS // 2, two_heads, 0)


def _topk(qp3, k1, k2):
    nh, t, qd = qp3.shape
    nc = t // LANES
    big = lambda: pl.BlockSpec((1, nh, PEER_KEYS, LANES), lambda i: (i, 0, 0, 0))
    keys = lambda: pl.BlockSpec((nh, PEER_KEYS, PEER_HALF), lambda i: (0, 0, 0))
    sds = jax.ShapeDtypeStruct((nc, nh, PEER_KEYS, LANES), F32)
    return pl.pallas_call(
        _topk_kernel,
        grid=(nc,),
        in_specs=[pl.BlockSpec((nh, LANES, qd), lambda i: (0, i, 0)), keys(), keys()],
        out_specs=[big(), big(), big(), big()],
        out_shape=[sds, sds, sds, sds],
        compiler_params=_params(("arbitrary",)),
        name="topk",
    )(qp3, k1, k2)


PEER_TILE = SUBLANES * PEER_KEYS


def _peer_kernel(ht_ref, u_ref, vt_ref, n1_ref, a1_ref, r2_ref, b2_ref, y_ref, acc, st, act,
                 *, nc):
    j = pl.program_id(1)

    @pl.when(j == 0)
    def _():
        acc[...] = jnp.zeros_like(acc)

    @pl.when(j >= 0)
    def _():
        st[...] = jnp.dot(u_ref[...], ht_ref[...], preferred_element_type=F32)

    @pl.when(j >= -1)
    def _():
        for c in range(nc):
            cs = slice(c * LANES, (c + 1) * LANES)
            for r in range(SUBLANES):
                w = jnp.zeros((PEER_KEYS, LANES), F32)
                for h in range(PEER_HEADS):
                    sel = jnp.where(r2_ref[c, h] < n1_ref[c, h, r:r + 1, :], b2_ref[c, h], 0.0)
                    w = w + sel * a1_ref[c, h, r:r + 1, :]
                rows = slice(r * PEER_KEYS, (r + 1) * PEER_KEYS)
                act[rows, cs] = (_gelu_tanh(st[rows, cs]) * w).astype(BF16)

    @pl.when(j >= -2)
    def _():
        acc[...] += jnp.dot(vt_ref[0], act[...], preferred_element_type=F32)

    @pl.when(j == pl.num_programs(1) - 1)
    def _():
        y_ref[...] = acc[...].T


def _peer(h2t, u_b, v_tiles, n1, a1, r2, b2, tm=512):
    d, t = h2t.shape
    te = PEER_TILE
    ne = u_b.shape[0] // te
    nh = n1.shape[1]
    nc = tm // LANES
    rows = lambda: pl.BlockSpec((nc, nh, SUBLANES, LANES), lambda i, j: (i, 0, j, 0))
    full = lambda: pl.BlockSpec((nc, nh, PEER_KEYS, LANES), lambda i, j: (i, 0, 0, 0),
                                pipeline_mode=pl.Buffered(1))
    return pl.pallas_call(
        functools.partial(_peer_kernel, nc=nc),
        grid=(t // tm, ne),
        in_specs=[pl.BlockSpec((d, tm), lambda i, j: (0, i)),
                  pl.BlockSpec((te, d), lambda i, j: (j, 0)),
                  pl.BlockSpec((1, d, te), lambda i, j: (j, 0, 0)),
                  rows(), rows(), full(), full()],
        out_specs=pl.BlockSpec((tm, d), lambda i, j: (i, 0)),
        out_shape=jax.ShapeDtypeStruct((t, d), F32),
        scratch_shapes=[pltpu.VMEM((d, tm), F32), pltpu.VMEM((te, tm), F32),
                        pltpu.VMEM((te, tm), BF16)],
        compiler_params=_params(("arbitrary", "arbitrary")),
        name="peer",
    )(h2t, u_b, v_tiles, n1, a1, r2, b2)


def _final_kernel(x1_ref, y_ref, g2_ref, gf_ref, o_ref):
    o_ref[...] = _rms(x1_ref[...] + g2_ref[0] * y_ref[...]) * gf_ref[...]


def _final(x1, y, ga2, g_final, seq, tm=512):
    t, d = x1.shape
    nb = seq // tm
    return pl.pallas_call(
        _final_kernel,
        grid=(t // tm,),
        in_specs=[pl.BlockSpec((tm, d), lambda i: (i, 0)),
                  pl.BlockSpec((tm, d), lambda i: (i, 0)),
                  pl.BlockSpec((1, 1, d), lambda i: (i // nb, 0, 0)),
                  pl.BlockSpec((1, d), lambda i: (0, 0))],
        out_specs=pl.BlockSpec((tm, d), lambda i: (i, 0)),
        out_shape=jax.ShapeDtypeStruct((t, d), F32),
        compiler_params=_params(("arbitrary",)),
        name="final",
    )(x1, y, ga2, g_final.reshape(1, d))


def kernel(x, c, w_ada, b_ada, g_mix, w_in, b_f, conv_w, conv_b, lru_w_a, lru_b_a, lru_w_x,
           lru_b_x, lru_lambda, g_attn_out, g_lru_out, w_out, g_ffn, peer_w_q, peer_k1,
           peer_k2, peer_u, peer_v, g_final):
    bsz, seq, d = x.shape
    depth = w_ada.shape[0]
    assert depth == 1, "the final residual + norm is fused for a single layer"
    x2 = x.reshape(bsz * seq, d)
    off_f = 3 * ATTN_WIDTH
    off_l = off_f + ATTN_HEADS

    for l in range(depth):
        mod = _ada(c, w_ada[l], b_ada[l])
        sh1, sc1, ga1, sh2, sc2, ga2 = [m.reshape(bsz, 1, d) for m in jnp.split(mod, N_ADA, axis=-1)]

        h = _modnorm(x2, g_mix[l], sc1, sh1, seq)
        w_l = w_in[l]
        qk = _matmul(h, w_l[:, :2 * ATTN_WIDTH].astype(BF16), BF16, name="proj_qk")
        vt = _matmul_t(h, w_l[:, 2 * ATTN_WIDTH:off_f].T.astype(BF16), BF16, tm=ATTN_TQ,
                       name="proj_vt")
        w_f = jnp.pad(w_l[:, off_f:off_l], ((0, 0), (0, LANES - ATTN_HEADS))).astype(BF16)
        fl = _matmul(h, w_f, F32, name="proj_f")
        lxy = _matmul(h, w_l[:, off_l:].astype(BF16), F32, name="proj_lru")

        bf = jnp.pad(b_f[l], (0, LANES - ATTN_HEADS)).reshape(1, LANES)
        cum_rep, cum_row = _cum(fl, bf, bsz, seq)
        attn = _attn(qk, vt, cum_rep, cum_row, bsz, seq)
        lru_n = _lru(lxy, conv_w[l], conv_b[l], lru_w_a[l], lru_b_a[l], lru_w_x[l], lru_b_x[l],
                     lru_lambda[l], g_lru_out[l], bsz, seq)

        w_o = w_out[l].astype(BF16)
        x2, h2, h2t = _outproj(attn, lru_n, x2, g_attn_out[l], w_o[:ATTN_WIDTH],
                               w_o[ATTN_WIDTH:], ga1, sc2, sh2, g_ffn[l], seq)

        qp3 = _matmul_heads(h2, peer_w_q[l].astype(BF16), PEER_HEADS, BF16, name="peer_q")
        n1, a1, r2, b2 = _topk(qp3, peer_k1[l].astype(BF16), peer_k2[l].astype(BF16))
        v_tiles = peer_v[l].reshape(-1, PEER_TILE, d).transpose(0, 2, 1).astype(BF16)
        y = _peer(h2t, peer_u[l].astype(BF16), v_tiles, n1, a1, r2, b2)
        x1 = x2
        x2 = None
    return _final(x1, y, ga2, g_final, seq).reshape(bsz, seq, d)
```

```python
import functools

import jax
import jax.numpy as jnp
from jax import lax
from jax.experimental import pallas as pl
from jax.experimental.pallas import tpu as pltpu

F32 = jnp.float32
BF16 = jnp.bfloat16

ATTN_HEADS = 8
HEAD_DIM = 128
ATTN_WIDTH = ATTN_HEADS * HEAD_DIM
LRU_BLOCKS = 8
LRU_BLOCK_DIM = 128
LRU_WIDTH = LRU_BLOCKS * LRU_BLOCK_DIM
CONV_WIDTH = 4
LRU_C = 8.0
PEER_HEADS = 8
PEER_KEYS = 128
PEER_HALF = 128
PEER_TOPK = 16
N_ADA = 6
EPS = 1e-6
LANES = 128
SUBLANES = 8
NEG_BIG = -1e30
LOG2E = 1.4426950408889634
VMEM_LIMIT = 56 * 1024 * 1024

_NT = (((1,), (1,)), ((), ()))


def _params(sem, vmem=VMEM_LIMIT):
    return pltpu.CompilerParams(dimension_semantics=sem, vmem_limit_bytes=vmem)


def _gelu_tanh(x):
    return 0.5 * x * (1.0 + jnp.tanh(0.7978845608028654 * (x + 0.044715 * (x * x * x))))


def _softplus(z):
    return jnp.maximum(z, 0.0) + jnp.log1p(jnp.exp(-jnp.abs(z)))


def _rms(x):
    return x * lax.rsqrt(jnp.mean(x * x, axis=-1, keepdims=True) + EPS)


def _ada_kernel(c_ref, w_ref, b_ref, o_ref):
    c = c_ref[...]
    sc = c * jax.nn.sigmoid(c)
    o_ref[...] = jnp.dot(sc.astype(BF16), w_ref[...].astype(BF16),
                         preferred_element_type=F32) + b_ref[...]


def _ada(c, w, b, tn=1024):
    bsz, d = c.shape
    n = w.shape[1]
    return pl.pallas_call(
        _ada_kernel,
        grid=(n // tn,),
        in_specs=[pl.BlockSpec((bsz, d), lambda j: (0, 0)),
                  pl.BlockSpec((d, tn), lambda j: (0, j)),
                  pl.BlockSpec((1, tn), lambda j: (0, j))],
        out_specs=pl.BlockSpec((bsz, tn), lambda j: (0, j)),
        out_shape=jax.ShapeDtypeStruct((bsz, n), F32),
        compiler_params=_params(("arbitrary",)),
        name="ada",
    )(c, w, b.reshape(1, n))


def _modnorm_kernel(x_ref, g_ref, sc_ref, sh_ref, o_ref):
    y = _rms(x_ref[...]) * g_ref[...]
    o_ref[...] = (y * (1.0 + sc_ref[0]) + sh_ref[0]).astype(o_ref.dtype)


def _modnorm(x2, g, sc, sh, seq, tm=512):
    t, d = x2.shape
    nb = seq // tm
    return pl.pallas_call(
        _modnorm_kernel,
        grid=(t // tm,),
        in_specs=[pl.BlockSpec((tm, d), lambda i: (i, 0)),
                  pl.BlockSpec((1, d), lambda i: (0, 0)),
                  pl.BlockSpec((1, 1, d), lambda i: (i // nb, 0, 0)),
                  pl.BlockSpec((1, 1, d), lambda i: (i // nb, 0, 0))],
        out_specs=pl.BlockSpec((tm, d), lambda i: (i, 0)),
        out_shape=jax.ShapeDtypeStruct((t, d), BF16),
        compiler_params=_params(("arbitrary",)),
        name="modnorm",
    )(x2, g.reshape(1, d), sc, sh)


def _matmul_kernel(a_ref, w_ref, o_ref):
    o_ref[...] = jnp.dot(a_ref[...], w_ref[...],
                         preferred_element_type=F32).astype(o_ref.dtype)


def _matmul(a, w, out_dtype, tm=512, tn=1024, name="matmul"):
    t, k = a.shape
    n = w.shape[1]
    tn = min(tn, n)
    return pl.pallas_call(
        _matmul_kernel,
        grid=(n // tn, t // tm),
        in_specs=[pl.BlockSpec((tm, k), lambda j, i: (i, 0)),
                  pl.BlockSpec((k, tn), lambda j, i: (0, j))],
        out_specs=pl.BlockSpec((tm, tn), lambda j, i: (i, j)),
        out_shape=jax.ShapeDtypeStruct((t, n), out_dtype),
        compiler_params=_params(("arbitrary", "arbitrary")),
        name=name,
    )(a, w)


def _matmul_colscale_kernel(a_ref, w_ref, cs_ref, o_ref):
    acc = jnp.dot(a_ref[...], w_ref[...], preferred_element_type=F32)
    o_ref[...] = (acc * cs_ref[...]).astype(o_ref.dtype)


def _matmul_colscale(a, w, col_scale, out_dtype, tm=512, tn=1024, name="matmul_colscale"):
    t, k = a.shape
    n = w.shape[1]
    return pl.pallas_call(
        _matmul_colscale_kernel,
        grid=(n // tn, t // tm),
        in_specs=[pl.BlockSpec((tm, k), lambda j, i: (i, 0)),
                  pl.BlockSpec((k, tn), lambda j, i: (0, j)),
                  pl.BlockSpec((1, tn), lambda j, i: (0, j))],
        out_specs=pl.BlockSpec((tm, tn), lambda j, i: (i, j)),
        out_shape=jax.ShapeDtypeStruct((t, n), out_dtype),
        compiler_params=_params(("arbitrary", "arbitrary")),
        name=name,
    )(a, w, col_scale.reshape(1, n))


def _matmul_heads_kernel(a_ref, w_ref, o_ref):
    res = jnp.dot(a_ref[...], w_ref[...], preferred_element_type=F32).astype(o_ref.dtype)
    width = o_ref.shape[2]
    for h in range(o_ref.shape[0]):
        o_ref[h] = res[:, h * width:(h + 1) * width]


def _matmul_heads(a, w, nh, out_dtype, tm=512, name="matmul_heads"):
    t, k = a.shape
    n = w.shape[1]
    return pl.pallas_call(
        _matmul_heads_kernel,
        grid=(t // tm,),
        in_specs=[pl.BlockSpec((tm, k), lambda i: (i, 0)),
                  pl.BlockSpec((k, n), lambda i: (0, 0))],
        out_specs=pl.BlockSpec((nh, tm, n // nh), lambda i: (0, i, 0)),
        out_shape=jax.ShapeDtypeStruct((nh, t, n // nh), out_dtype),
        compiler_params=_params(("arbitrary",)),
        name=name,
    )(a, w)


def _matmul_t_kernel(a_ref, wt_ref, o_ref):
    o_ref[0] = lax.dot_general(wt_ref[...], a_ref[...], _NT,
                               preferred_element_type=F32).astype(o_ref.dtype)


def _matmul_t(a, wt, out_dtype, tm=512, tn=1024, name="matmul_t"):
    t, k = a.shape
    n = wt.shape[0]
    tn = min(tn, n)
    return pl.pallas_call(
        _matmul_t_kernel,
        grid=(n // tn, t // tm),
        in_specs=[pl.BlockSpec((tm, k), lambda j, i: (i, 0)),
                  pl.BlockSpec((tn, k), lambda j, i: (j, 0))],
        out_specs=pl.BlockSpec((1, tn, tm), lambda j, i: (i, j, 0)),
        out_shape=jax.ShapeDtypeStruct((t // tm, n, tm), out_dtype),
        compiler_params=_params(("arbitrary", "arbitrary")),
        name=name,
    )(a, wt)


def _cum_kernel(fl_ref, bf_ref, rep_ref, row_ref, carry_ref, *, ts):
    @pl.when(pl.program_id(1) == 0)
    def _():
        carry_ref[...] = jnp.zeros_like(carry_ref)

    lf = -_softplus(-(fl_ref[...] + bf_ref[...]))
    rows = lax.broadcasted_iota(jnp.int32, lf.shape, 0)
    s = 1
    while s < ts:
        lf = lf + jnp.where(rows >= s, pltpu.roll(lf, s, axis=0), 0.0)
        s *= 2
    cum = lf + carry_ref[...]
    carry_ref[...] = cum[ts - 1:ts, :]
    cum2 = cum * LOG2E
    for h in range(ATTN_HEADS):
        rep_ref[0, h] = jnp.broadcast_to(cum2[:, h:h + 1], (ts, LANES))
    row_ref[0] = cum2.T[0:ATTN_HEADS, :]


def _cum(fl, bf, bsz, seq, ts=512):
    ns = seq // ts
    return pl.pallas_call(
        functools.partial(_cum_kernel, ts=ts),
        grid=(bsz, ns),
        in_specs=[pl.BlockSpec((ts, LANES), lambda b, s: (b * ns + s, 0)),
                  pl.BlockSpec((1, LANES), lambda b, s: (0, 0))],
        out_specs=[pl.BlockSpec((1, ATTN_HEADS, ts, LANES), lambda b, s: (b, 0, s, 0)),
                   pl.BlockSpec((1, ATTN_HEADS, ts), lambda b, s: (b, 0, s))],
        out_shape=[jax.ShapeDtypeStruct((bsz, ATTN_HEADS, seq, LANES), F32),
                   jax.ShapeDtypeStruct((bsz, ATTN_HEADS, seq), F32)],
        scratch_shapes=[pltpu.VMEM((1, LANES), F32)],
        compiler_params=_params(("arbitrary", "arbitrary")),
        name="cum",
    )(fl, bf)


def _attn_kernel(q_ref, k_ref, vt_ref, cq_ref, ck_ref, o_ref, m_ref, l_ref, acc_ref, *, tq):
    qi = pl.program_id(2)
    q = q_ref[...]
    cq = cq_ref[0, 0, 0]
    m_ref[...] = jnp.full_like(m_ref, NEG_BIG)
    l_ref[...] = jnp.zeros_like(l_ref)
    acc_ref[...] = jnp.zeros_like(acc_ref)
    reps = tq // LANES

    def step(j, masked):
        ks = pl.multiple_of(j * tq, tq)
        k = k_ref[pl.ds(ks, tq), :]
        ck = ck_ref[0, 0, pl.ds(ks, tq), :]
        s = lax.dot_general(k, q, _NT, preferred_element_type=F32)
        s = s + (cq - jnp.concatenate([ck] * reps, axis=1))
        if masked:
            key = lax.broadcasted_iota(jnp.int32, (tq, tq), 0)
            qry = lax.broadcasted_iota(jnp.int32, (tq, tq), 1)
            s = jnp.where(key <= qry, s, NEG_BIG)
        m_prev = m_ref[...]
        m_new = jnp.maximum(m_prev, jnp.max(s, axis=0, keepdims=True))
        alpha = jnp.exp2(m_prev - m_new)
        p = jnp.exp2(s - m_new)
        l_ref[...] = alpha * l_ref[...] + jnp.sum(p, axis=0, keepdims=True)
        acc_ref[...] = alpha * acc_ref[...] + jnp.dot(vt_ref[j], p.astype(BF16),
                                                      preferred_element_type=F32)
        m_ref[...] = m_new

    def body(j, carry):
        step(j, False)
        return carry

    lax.fori_loop(0, qi, body, 0)
    step(qi, True)
    o_ref[...] = (acc_ref[...] / l_ref[...]).T


ATTN_TQ = 512


def _attn(qk, vt, cum_rep, cum_row, bsz, seq, tq=ATTN_TQ):
    t = qk.shape[0]
    nq = seq // tq
    h = ATTN_HEADS
    cq = cum_row.reshape(bsz, h, nq, 1, tq)
    return pl.pallas_call(
        functools.partial(_attn_kernel, tq=tq),
        grid=(bsz, h, nq),
        in_specs=[pl.BlockSpec((tq, HEAD_DIM), lambda b, hh, i: (b * nq + i, hh)),
                  pl.BlockSpec((seq, HEAD_DIM), lambda b, hh, i: (b, h + hh)),
                  pl.BlockSpec((nq, HEAD_DIM, tq), lambda b, hh, i: (b, hh, 0)),
                  pl.BlockSpec((1, 1, 1, 1, tq), lambda b, hh, i: (b, hh, i, 0, 0)),
                  pl.BlockSpec((1, 1, seq, LANES), lambda b, hh, i: (b, hh, 0, 0))],
        out_specs=pl.BlockSpec((tq, HEAD_DIM), lambda b, hh, i: (b * nq + i, hh)),
        out_shape=jax.ShapeDtypeStruct((t, ATTN_WIDTH), F32),
        scratch_shapes=[pltpu.VMEM((1, tq), F32), pltpu.VMEM((1, tq), F32),
                        pltpu.VMEM((HEAD_DIM, tq), F32)],
        compiler_params=_params(("arbitrary", "arbitrary", "arbitrary")),
        name="attn",
    )(qk, qk, vt, cq, cum_rep)


def _lru_kernel(lx_ref, ly_ref, cw_ref, cb_ref, wa_ref, ba_ref, wx_ref, bx_ref, lam_ref,
                g_ref, o_ref, xbuf, hc, *, ts):
    @pl.when(pl.program_id(1) == 0)
    def _():
        xbuf[0:SUBLANES, :] = jnp.zeros((SUBLANES, LRU_WIDTH), F32)
        hc[...] = jnp.zeros_like(hc)

    xbuf[SUBLANES:, :] = lx_ref[...]
    xc = cb_ref[...]
    for k in range(CONV_WIDTH):
        off = SUBLANES - (CONV_WIDTH - 1) + k
        xc = xc + cw_ref[k:k + 1, :] * xbuf[off:off + ts, :]
    xbuf[0:SUBLANES, :] = xbuf[ts:ts + SUBLANES, :]

    xcb = xc.astype(BF16)
    ra, rx = [], []
    for n in range(LRU_BLOCKS):
        xs = xcb[:, n * LRU_BLOCK_DIM:(n + 1) * LRU_BLOCK_DIM]
        ra.append(jnp.dot(xs, wa_ref[n], preferred_element_type=F32))
        rx.append(jnp.dot(xs, wx_ref[n], preferred_element_type=F32))
    r = jax.nn.sigmoid(jnp.concatenate(ra, axis=1) + ba_ref[...])
    ig = jax.nn.sigmoid(jnp.concatenate(rx, axis=1) + bx_ref[...])
    log_a = (-LRU_C) * r * _softplus(-lam_ref[...])
    a = jnp.exp(log_a)
    b = jnp.sqrt(-jnp.tanh(log_a) * (a * a + 1.0)) * (ig * xc)

    rows = lax.broadcasted_iota(jnp.int32, a.shape, 0)
    s = 1
    while s < ts:
        keep = rows >= s
        b = jnp.where(keep, a * pltpu.roll(b, s, axis=0) + b, b)
        a = jnp.where(keep, a * pltpu.roll(a, s, axis=0), a)
        s *= 2
    hseq = b + a * hc[...]
    hc[...] = hseq[ts - 1:ts, :]

    lru = hseq * _gelu_tanh(ly_ref[...])
    o_ref[...] = (_rms(lru) * g_ref[...]).astype(o_ref.dtype)


def _lru(lxy, conv_w, conv_b, w_a, b_a, w_x, b_x, lam, g, bsz, seq, ts=256):
    t = lxy.shape[0]
    ns = seq // ts
    w = LRU_WIDTH
    vec = lambda: pl.BlockSpec((1, w), lambda b, s: (0, 0))
    blk = lambda: pl.BlockSpec((LRU_BLOCKS, LRU_BLOCK_DIM, LRU_BLOCK_DIM), lambda b, s: (0, 0, 0))
    return pl.pallas_call(
        functools.partial(_lru_kernel, ts=ts),
        grid=(bsz, ns),
        in_specs=[pl.BlockSpec((ts, w), lambda b, s: (b * ns + s, 0)),
                  pl.BlockSpec((ts, w), lambda b, s: (b * ns + s, 1)),
                  pl.BlockSpec((CONV_WIDTH, w), lambda b, s: (0, 0)),
                  vec(), blk(), vec(), blk(), vec(), vec(), vec()],
        out_specs=pl.BlockSpec((ts, w), lambda b, s: (b * ns + s, 0)),
        out_shape=jax.ShapeDtypeStruct((t, w), BF16),
        scratch_shapes=[pltpu.VMEM((ts + SUBLANES, w), F32), pltpu.VMEM((1, w), F32)],
        compiler_params=_params(("arbitrary", "arbitrary")),
        name="lru",
    )(lxy, lxy, conv_w, conv_b.reshape(1, w), w_a.astype(BF16), b_a.reshape(1, w),
      w_x.astype(BF16), b_x.reshape(1, w), lam.reshape(1, w), g.reshape(1, w))


def _outproj_kernel(at_ref, lr_ref, x_ref, ga_ref, woa_ref, wol_ref, g1_ref, sc_ref, sh_ref,
                    gf_ref, x1_ref, h2_ref, h2t_ref):
    an = (_rms(at_ref[...]) * ga_ref[...]).astype(BF16)
    mix = jnp.dot(an, woa_ref[...], preferred_element_type=F32)
    mix = mix + jnp.dot(lr_ref[...], wol_ref[...], preferred_element_type=F32)
    x1 = x_ref[...] + g1_ref[0] * mix
    x1_ref[...] = x1
    h2 = (_rms(x1) * gf_ref[...]) * (1.0 + sc_ref[0]) + sh_ref[0]
    h2_ref[...] = h2.astype(h2_ref.dtype)
    h2t_ref[...] = h2.T.astype(h2t_ref.dtype)


def _outproj(attn, lru_n, x2, g_attn, w_oa, w_ol, ga1, sc2, sh2, g_ffn, seq, tm=256):
    t, d = x2.shape
    nb = seq // tm
    aw = attn.shape[1]
    lw = lru_n.shape[1]
    mod = lambda: pl.BlockSpec((1, 1, d), lambda i: (i // nb, 0, 0))
    return pl.pallas_call(
        _outproj_kernel,
        grid=(t // tm,),
        in_specs=[pl.BlockSpec((tm, aw), lambda i: (i, 0)),
                  pl.BlockSpec((tm, lw), lambda i: (i, 0)),
                  pl.BlockSpec((tm, d), lambda i: (i, 0)),
                  pl.BlockSpec((1, aw), lambda i: (0, 0)),
                  pl.BlockSpec((aw, d), lambda i: (0, 0)),
                  pl.BlockSpec((lw, d), lambda i: (0, 0)),
                  mod(), mod(), mod(),
                  pl.BlockSpec((1, d), lambda i: (0, 0))],
        out_specs=[pl.BlockSpec((tm, d), lambda i: (i, 0)),
                   pl.BlockSpec((tm, d), lambda i: (i, 0)),
                   pl.BlockSpec((d, tm), lambda i: (0, i))],
        out_shape=[jax.ShapeDtypeStruct((t, d), F32), jax.ShapeDtypeStruct((t, d), BF16),
                   jax.ShapeDtypeStruct((d, t), BF16)],
        compiler_params=_params(("arbitrary",)),
        name="outproj",
    )(attn, lru_n, x2, g_attn.reshape(1, aw), w_oa, w_ol, ga1, sc2, sh2, g_ffn.reshape(1, d))


def _sort16_pairs():
    pairs = []

    def merge(lo, hi, r):
        step = r * 2
        if step < hi - lo:
            merge(lo, hi, step)
            merge(lo + r, hi, step)
            for i in range(lo + r, hi - r, step):
                pairs.append((i, i + r))
        else:
            pairs.append((lo, lo + r))

    def sort(lo, hi):
        if hi - lo >= 1:
            mid = lo + (hi - lo) // 2
            sort(lo, mid)
            sort(mid + 1, hi)
            merge(lo, hi, 1)

    sort(0, PEER_TOPK - 1)
    return pairs


_SORT16 = _sort16_pairs()


def _cex(v, i, j):
    hi, lo = jnp.maximum(v[i], v[j]), jnp.minimum(v[i], v[j])
    v[i], v[j] = hi, lo


def _sort16(v):
    v = list(v)
    for i, j in _SORT16:
        _cex(v, i, j)
    return v


def _bitonic_merge16(v):
    v = list(v)
    d = PEER_TOPK // 2
    while d >= 1:
        for i in range(PEER_TOPK):
            if i & d == 0:
                _cex(v, i, i + d)
        d //= 2
    return v


def _top16_of_union(a, b):
    return [jnp.maximum(a[i], b[PEER_TOPK - 1 - i]) for i in range(PEER_TOPK)]


def _top16_sorted(s):
    v = _sort16([s[i * SUBLANES:(i + 1) * SUBLANES, :] for i in range(PEER_TOPK)])
    for shift in (4, 2, 1):
        other = [pltpu.roll(x, shift, axis=0) for x in v]
        v = _bitonic_merge16(_top16_of_union(v, other))
    return v


def _threshold_stats(v1, v2):
    k = PEER_TOPK
    cells = {(i, j): v1[i] + v2[j] for i in range(k) for j in range(k) if (i + 1) * (j + 1) <= k}
    row0 = [cells[0, j] for j in range(k)]
    col0 = [cells[i, 0] for i in range(1, k)]
    rest = ([(1, j) for j in range(1, 8)] + [(i, 1) for i in range(2, 8)]
            + [(2, j) for j in range(2, 5)] + [(i, 2) for i in range(3, 5)] + [(3, 3)])
    rest = [cells[ij] for ij in rest]
    ab = [row0[0]] + [jnp.maximum(row0[i], col0[k - 1 - i]) for i in range(1, k)]
    ab = _bitonic_merge16(ab)
    c1 = _sort16(rest[:k])
    t = _bitonic_merge16(_top16_of_union(ab, c1))
    y = list(rest[k:])
    _cex(y, 0, 1), _cex(y, 1, 2), _cex(y, 0, 1)
    thr = jnp.minimum(jnp.minimum(t[12], jnp.maximum(t[13], y[2])),
                      jnp.minimum(jnp.maximum(t[14], y[1]), jnp.maximum(t[15], y[0])))
    top = row0[0]
    z = jnp.zeros_like(top)
    for c in cells.values():
        z = z + jnp.where(c >= thr, jnp.exp(c - top), 0.0)
    t = []
    for j in range(k):
        tj = jnp.full_like(top, jnp.inf)
        for i in range(k // (j + 1)):
            tj = jnp.where(cells[i, j] >= thr, v1[i], tj)
        t.append(tj)
    return z, t


def _topk_kernel(qp_ref, k1_ref, k2_ref, c1_ref, a1_ref, s2_ref, b2_ref):
    reps = PEER_KEYS // SUBLANES

    def head(h):
        q = qp_ref[h]
        s1 = lax.dot_general(k1_ref[h], q[:, :PEER_HALF], _NT, preferred_element_type=F32)
        s2 = lax.dot_general(k2_ref[h], q[:, PEER_HALF:], _NT, preferred_element_type=F32)
        v1 = _top16_sorted(s1)
        v2 = _top16_sorted(s2)
        z, t = _threshold_stats(v1, v2)
        c1 = jnp.full_like(s1, jnp.inf)
        for tj, v in zip(t, v2):
            c1 = jnp.where(s1 >= jnp.tile(tj, (reps, 1)), jnp.tile(v, (reps, 1)), c1)
        c1_ref[0, h] = c1
        s2_ref[0, h] = s2
        a1_ref[0, h] = jnp.exp(s1 - v1[0][0:1, :]) / z[0:1, :]
        b2_ref[0, h] = jnp.exp(s2 - v2[0][0:1, :])

    def two_heads(i, carry):
        head(2 * i)
        head(2 * i + 1)
        return carry

    lax.fori_loop(0, PEER_HEADS // 2, two_heads, 0)


def _topk(qp3, k1, k2):
    nh, t, qd = qp3.shape
    nc = t // LANES
    big = lambda: pl.BlockSpec((1, nh, PEER_KEYS, LANES), lambda i: (i, 0, 0, 0))
    keys = lambda: pl.BlockSpec((nh, PEER_KEYS, PEER_HALF), lambda i: (0, 0, 0))
    sds = jax.ShapeDtypeStruct((nc, nh, PEER_KEYS, LANES), F32)
    return pl.pallas_call(
        _topk_kernel,
        grid=(nc,),
        in_specs=[pl.BlockSpec((nh, LANES, qd), lambda i: (0, i, 0)), keys(), keys()],
        out_specs=[big(), big(), big(), big()],
        out_shape=[sds, sds, sds, sds],
        compiler_params=_params(("arbitrary",)),
        name="topk",
    )(qp3, k1, k2)


PEER_TILE = SUBLANES * PEER_KEYS


def _peer_kernel(ht_ref, u_ref, vt_ref, c1_ref, a1_ref, s2_ref, b2_ref, x1_ref, g2_ref, gf_ref,
                 o_ref, acc, st, act, *, nc):
    j = pl.program_id(1)

    @pl.when(j == 0)
    def _():
        acc[...] = jnp.zeros_like(acc)

    @pl.when(j >= 0)
    def _():
        st[...] = jnp.dot(u_ref[...], ht_ref[...], preferred_element_type=F32)

    @pl.when(j >= -1)
    def _():
        for c in range(nc):
            cs = slice(c * LANES, (c + 1) * LANES)
            for r in range(SUBLANES):
                w = jnp.zeros((PEER_KEYS, LANES), F32)
                for h in range(PEER_HEADS):
                    sel = jnp.where(s2_ref[c, h] >= c1_ref[c, h, r:r + 1, :], b2_ref[c, h], 0.0)
                    w = w + sel * a1_ref[c, h, r:r + 1, :]
                rows = slice(r * PEER_KEYS, (r + 1) * PEER_KEYS)
                act[rows, cs] = (_gelu_tanh(st[rows, cs]) * w).astype(BF16)

    @pl.when(j >= -2)
    def _():
        acc[...] += jnp.dot(vt_ref[0], act[...], preferred_element_type=F32)

    @pl.when(j == pl.num_programs(1) - 1)
    def _():
        o_ref[...] = _rms(x1_ref[...] + g2_ref[0] * acc[...].T) * gf_ref[...]


def _peer(h2t, u_b, v_tiles, c1, a1, s2, b2, x1, ga2, g_final, seq, tm=512):
    d, t = h2t.shape
    te = PEER_TILE
    ne = u_b.shape[0] // te
    nh = c1.shape[1]
    nc = tm // LANES
    nb = seq // tm
    rows = lambda: pl.BlockSpec((nc, nh, SUBLANES, LANES), lambda i, j: (i, 0, j, 0))
    full = lambda: pl.BlockSpec((nc, nh, PEER_KEYS, LANES), lambda i, j: (i, 0, 0, 0),
                                pipeline_mode=pl.Buffered(1))
    return pl.pallas_call(
        functools.partial(_peer_kernel, nc=nc),
        grid=(t // tm, ne),
        in_specs=[pl.BlockSpec((d, tm), lambda i, j: (0, i)),
                  pl.BlockSpec((te, d), lambda i, j: (j, 0)),
                  pl.BlockSpec((1, d, te), lambda i, j: (j, 0, 0)),
                  rows(), rows(), full(), full(),
                  pl.BlockSpec((tm, d), lambda i, j: (i, 0)),
                  pl.BlockSpec((1, 1, d), lambda i, j: (i // nb, 0, 0)),
                  pl.BlockSpec((1, d), lambda i, j: (0, 0))],
        out_specs=pl.BlockSpec((tm, d), lambda i, j: (i, 0)),
        out_shape=jax.ShapeDtypeStruct((t, d), F32),
        scratch_shapes=[pltpu.VMEM((d, tm), F32), pltpu.VMEM((te, tm), F32),
                        pltpu.VMEM((te, tm), BF16)],
        compiler_params=_params(("arbitrary", "arbitrary")),
        name="peer",
    )(h2t, u_b, v_tiles, c1, a1, s2, b2, x1, ga2, g_final.reshape(1, d))


def kernel(x, c, w_ada, b_ada, g_mix, w_in, b_f, conv_w, conv_b, lru_w_a, lru_b_a, lru_w_x,
           lru_b_x, lru_lambda, g_attn_out, g_lru_out, w_out, g_ffn, peer_w_q, peer_k1,
           peer_k2, peer_u, peer_v, g_final):
    bsz, seq, d = x.shape
    depth = w_ada.shape[0]
    assert depth == 1, "the final residual + norm is fused into the PEER kernel of a single layer"
    x2 = x.reshape(bsz * seq, d)
    off_f = 3 * ATTN_WIDTH
    off_l = off_f + ATTN_HEADS

    for l in range(depth):
        mod = _ada(c, w_ada[l], b_ada[l])
        sh1, sc1, ga1, sh2, sc2, ga2 = [m.reshape(bsz, 1, d) for m in jnp.split(mod, N_ADA, axis=-1)]

        h = _modnorm(x2, g_mix[l], sc1, sh1, seq)
        w_l = w_in[l]
        q_scale = jnp.where(jnp.arange(2 * ATTN_WIDTH) < ATTN_WIDTH, LOG2E * HEAD_DIM ** -0.5, 1.0)
        qk = _matmul_colscale(h, w_l[:, :2 * ATTN_WIDTH].astype(BF16), q_scale.astype(F32), BF16,
                              name="proj_qk")
        vt = _matmul_t(h, w_l[:, 2 * ATTN_WIDTH:off_f].T.astype(BF16), BF16, tm=ATTN_TQ,
                       name="proj_vt")
        w_f = jnp.pad(w_l[:, off_f:off_l], ((0, 0), (0, LANES - ATTN_HEADS))).astype(BF16)
        fl = _matmul(h, w_f, F32, name="proj_f")
        lxy = _matmul(h, w_l[:, off_l:].astype(BF16), F32, name="proj_lru")

        bf = jnp.pad(b_f[l], (0, LANES - ATTN_HEADS)).reshape(1, LANES)
        cum_rep, cum_row = _cum(fl, bf, bsz, seq)
        attn = _attn(qk, vt, cum_rep, cum_row, bsz, seq)
        lru_n = _lru(lxy, conv_w[l], conv_b[l], lru_w_a[l], lru_b_a[l], lru_w_x[l], lru_b_x[l],
                     lru_lambda[l], g_lru_out[l], bsz, seq)

        w_o = w_out[l].astype(BF16)
        x1, h2, h2t = _outproj(attn, lru_n, x2, g_attn_out[l], w_o[:ATTN_WIDTH],
                               w_o[ATTN_WIDTH:], ga1, sc2, sh2, g_ffn[l], seq)

        qp3 = _matmul_heads(h2, peer_w_q[l].astype(BF16), PEER_HEADS, BF16, name="peer_q")
        c1, a1, s2, b2 = _topk(qp3, peer_k1[l].astype(BF16), peer_k2[l].astype(BF16))
        v_tiles = peer_v[l].reshape(-1, PEER_TILE, d).transpose(0, 2, 1).astype(BF16)
        out = _peer(h2t, peer_u[l].astype(BF16), v_tiles, c1, a1, s2, b2, x1, ga2, g_final, seq)
    return out.reshape(bsz, seq, d)
```

```python
import functools

import jax
import jax.numpy as jnp
from jax import lax
from jax.experimental import pallas as pl
from jax.experimental.pallas import tpu as pltpu

F32 = jnp.float32
BF16 = jnp.bfloat16

ATTN_HEADS = 8
HEAD_DIM = 128
ATTN_WIDTH = ATTN_HEADS * HEAD_DIM
LRU_BLOCKS = 8
LRU_BLOCK_DIM = 128
LRU_WIDTH = LRU_BLOCKS * LRU_BLOCK_DIM
CONV_WIDTH = 4
LRU_C = 8.0
PEER_HEADS = 8
PEER_KEYS = 128
PEER_HALF = 128
PEER_TOPK = 16
N_ADA = 6
EPS = 1e-6
LANES = 128
SUBLANES = 8
NEG_BIG = -1e30
LOG2E = 1.4426950408889634
VMEM_LIMIT = 56 * 1024 * 1024

_NT = (((1,), (1,)), ((), ()))


def _params(sem, vmem=VMEM_LIMIT):
    return pltpu.CompilerParams(dimension_semantics=sem, vmem_limit_bytes=vmem)


def _gelu_tanh(x):
    return 0.5 * x * (1.0 + jnp.tanh(0.7978845608028654 * (x + 0.044715 * (x * x * x))))


def _softplus(z):
    return jnp.maximum(z, 0.0) + jnp.log1p(jnp.exp(-jnp.abs(z)))


def _rms(x):
    return x * lax.rsqrt(jnp.mean(x * x, axis=-1, keepdims=True) + EPS)


def _ada_kernel(c_ref, w_ref, b_ref, o_ref):
    @pl.when(pl.program_id(0) == 0)
    def _():
        o_ref[...] = jnp.broadcast_to(b_ref[...], o_ref.shape)

    c = c_ref[...]
    sc = c * jax.nn.sigmoid(c)
    o_ref[...] += jnp.dot(sc.astype(BF16), w_ref[...].astype(BF16), preferred_element_type=F32)


def _ada(c, w, b, tk=256):
    bsz, d = c.shape
    n = w.shape[1]
    return pl.pallas_call(
        _ada_kernel,
        grid=(d // tk,),
        in_specs=[pl.BlockSpec((bsz, tk), lambda k: (0, k)),
                  pl.BlockSpec((tk, n), lambda k: (k, 0)),
                  pl.BlockSpec((1, n), lambda k: (0, 0))],
        out_specs=pl.BlockSpec((bsz, n), lambda k: (0, 0)),
        out_shape=jax.ShapeDtypeStruct((bsz, n), F32),
        compiler_params=_params(("arbitrary",)),
        name="ada",
    )(c, w, b.reshape(1, n))


def _modnorm_kernel(x_ref, g_ref, sc_ref, sh_ref, o_ref):
    y = _rms(x_ref[...]) * g_ref[...]
    o_ref[...] = (y * (1.0 + sc_ref[0]) + sh_ref[0]).astype(o_ref.dtype)


def _modnorm(x2, g, sc, sh, seq, tm=512):
    t, d = x2.shape
    nb = seq // tm
    return pl.pallas_call(
        _modnorm_kernel,
        grid=(t // tm,),
        in_specs=[pl.BlockSpec((tm, d), lambda i: (i, 0)),
                  pl.BlockSpec((1, d), lambda i: (0, 0)),
                  pl.BlockSpec((1, 1, d), lambda i: (i // nb, 0, 0)),
                  pl.BlockSpec((1, 1, d), lambda i: (i // nb, 0, 0))],
        out_specs=pl.BlockSpec((tm, d), lambda i: (i, 0)),
        out_shape=jax.ShapeDtypeStruct((t, d), BF16),
        compiler_params=_params(("arbitrary",)),
        name="modnorm",
    )(x2, g.reshape(1, d), sc, sh)


def _matmul_kernel(a_ref, w_ref, o_ref):
    o_ref[...] = jnp.dot(a_ref[...], w_ref[...],
                         preferred_element_type=F32).astype(o_ref.dtype)


def _matmul(a, w, out_dtype, tm=512, tn=1024, name="matmul"):
    t, k = a.shape
    n = w.shape[1]
    tn = min(tn, n)
    return pl.pallas_call(
        _matmul_kernel,
        grid=(n // tn, t // tm),
        in_specs=[pl.BlockSpec((tm, k), lambda j, i: (i, 0)),
                  pl.BlockSpec((k, tn), lambda j, i: (0, j))],
        out_specs=pl.BlockSpec((tm, tn), lambda j, i: (i, j)),
        out_shape=jax.ShapeDtypeStruct((t, n), out_dtype),
        compiler_params=_params(("arbitrary", "arbitrary")),
        name=name,
    )(a, w)


def _matmul_colscale_kernel(a_ref, w_ref, cs_ref, o_ref):
    acc = jnp.dot(a_ref[...], w_ref[...], preferred_element_type=F32)
    o_ref[...] = (acc * cs_ref[...]).astype(o_ref.dtype)


def _matmul_colscale(a, w, col_scale, out_dtype, tm=512, tn=1024, name="matmul_colscale"):
    t, k = a.shape
    n = w.shape[1]
    return pl.pallas_call(
        _matmul_colscale_kernel,
        grid=(n // tn, t // tm),
        in_specs=[pl.BlockSpec((tm, k), lambda j, i: (i, 0)),
                  pl.BlockSpec((k, tn), lambda j, i: (0, j)),
                  pl.BlockSpec((1, tn), lambda j, i: (0, j))],
        out_specs=pl.BlockSpec((tm, tn), lambda j, i: (i, j)),
        out_shape=jax.ShapeDtypeStruct((t, n), out_dtype),
        compiler_params=_params(("arbitrary", "arbitrary")),
        name=name,
    )(a, w, col_scale.reshape(1, n))


def _matmul_heads_kernel(a_ref, w_ref, o_ref):
    res = jnp.dot(a_ref[...], w_ref[...], preferred_element_type=F32).astype(o_ref.dtype)
    width = o_ref.shape[2]
    for h in range(o_ref.shape[0]):
        o_ref[h] = res[:, h * width:(h + 1) * width]


def _matmul_heads(a, w, nh, out_dtype, tm=512, name="matmul_heads"):
    t, k = a.shape
    n = w.shape[1]
    return pl.pallas_call(
        _matmul_heads_kernel,
        grid=(t // tm,),
        in_specs=[pl.BlockSpec((tm, k), lambda i: (i, 0)),
                  pl.BlockSpec((k, n), lambda i: (0, 0))],
        out_specs=pl.BlockSpec((nh, tm, n // nh), lambda i: (0, i, 0)),
        out_shape=jax.ShapeDtypeStruct((nh, t, n // nh), out_dtype),
        compiler_params=_params(("arbitrary",)),
        name=name,
    )(a, w)


def _matmul_t_kernel(a_ref, wt_ref, o_ref):
    o_ref[0] = lax.dot_general(wt_ref[...], a_ref[...], _NT,
                               preferred_element_type=F32).astype(o_ref.dtype)


def _matmul_t(a, wt, out_dtype, tm=512, tn=1024, name="matmul_t"):
    t, k = a.shape
    n = wt.shape[0]
    tn = min(tn, n)
    return pl.pallas_call(
        _matmul_t_kernel,
        grid=(n // tn, t // tm),
        in_specs=[pl.BlockSpec((tm, k), lambda j, i: (i, 0)),
                  pl.BlockSpec((tn, k), lambda j, i: (j, 0))],
        out_specs=pl.BlockSpec((1, tn, tm), lambda j, i: (i, j, 0)),
        out_shape=jax.ShapeDtypeStruct((t // tm, n, tm), out_dtype),
        compiler_params=_params(("arbitrary", "arbitrary")),
        name=name,
    )(a, wt)


def _cum_kernel(fl_ref, bf_ref, rep_ref, row_ref, carry_ref, *, ts):
    @pl.when(pl.program_id(1) == 0)
    def _():
        carry_ref[...] = jnp.zeros_like(carry_ref)

    lf = -_softplus(-(fl_ref[...] + bf_ref[...]))
    rows = lax.broadcasted_iota(jnp.int32, lf.shape, 0)
    s = 1
    while s < ts:
        lf = lf + jnp.where(rows >= s, pltpu.roll(lf, s, axis=0), 0.0)
        s *= 2
    cum = lf + carry_ref[...]
    carry_ref[...] = cum[ts - 1:ts, :]
    cum2 = cum * LOG2E
    for h in range(ATTN_HEADS):
        rep_ref[0, h] = jnp.broadcast_to(cum2[:, h:h + 1], (ts, LANES))
    row_ref[0] = cum2.T[0:ATTN_HEADS, :]


def _cum(fl, bf, bsz, seq, ts=512):
    ns = seq // ts
    return pl.pallas_call(
        functools.partial(_cum_kernel, ts=ts),
        grid=(bsz, ns),
        in_specs=[pl.BlockSpec((ts, LANES), lambda b, s: (b * ns + s, 0)),
                  pl.BlockSpec((1, LANES), lambda b, s: (0, 0))],
        out_specs=[pl.BlockSpec((1, ATTN_HEADS, ts, LANES), lambda b, s: (b, 0, s, 0)),
                   pl.BlockSpec((1, ATTN_HEADS, ts), lambda b, s: (b, 0, s))],
        out_shape=[jax.ShapeDtypeStruct((bsz, ATTN_HEADS, seq, LANES), F32),
                   jax.ShapeDtypeStruct((bsz, ATTN_HEADS, seq), F32)],
        scratch_shapes=[pltpu.VMEM((1, LANES), F32)],
        compiler_params=_params(("arbitrary", "arbitrary")),
        name="cum",
    )(fl, bf)


def _attn_kernel(q_ref, k_ref, vt_ref, cq_ref, ck_ref, o_ref, m_ref, l_ref, acc_ref, *, tq):
    qi = pl.program_id(2)
    q = q_ref[...]
    cq = cq_ref[0, 0, 0]
    m_ref[...] = jnp.full_like(m_ref, NEG_BIG)
    l_ref[...] = jnp.zeros_like(l_ref)
    acc_ref[...] = jnp.zeros_like(acc_ref)
    reps = tq // LANES

    def step(j, nblk, masked):
        tk = nblk * tq
        ks = pl.multiple_of(j * tq, tq)
        k = k_ref[pl.ds(ks, tk), :]
        ck = ck_ref[0, 0, pl.ds(ks, tk), :]
        s = lax.dot_general(k, q, _NT, preferred_element_type=F32)
        s = s + (cq - jnp.concatenate([ck] * reps, axis=1))
        if masked:
            key = lax.broadcasted_iota(jnp.int32, (tk, tq), 0)
            qry = lax.broadcasted_iota(jnp.int32, (tk, tq), 1)
            s = jnp.where(key <= qry, s, NEG_BIG)
        m_prev = m_ref[...]
        m_new = jnp.maximum(m_prev, jnp.max(s, axis=0, keepdims=True))
        alpha = jnp.exp2(m_prev - m_new)
        p = jnp.exp2(s - m_new)
        l_ref[...] = alpha * l_ref[...] + jnp.sum(p, axis=0, keepdims=True)
        pb = p.astype(BF16)
        pv = jnp.dot(vt_ref[j], pb[0:tq], preferred_element_type=F32)
        for b in range(1, nblk):
            pv = pv + jnp.dot(vt_ref[j + b], pb[b * tq:(b + 1) * tq], preferred_element_type=F32)
        acc_ref[...] = alpha * acc_ref[...] + pv
        m_ref[...] = m_new

    def pair(i, carry):
        step(2 * i, 2, False)
        return carry

    lax.fori_loop(0, qi // 2, pair, 0)

    @pl.when(qi % 2 == 1)
    def _():
        step(qi - 1, 1, False)

    step(qi, 1, True)
    o_ref[...] = (acc_ref[...] / l_ref[...]).T


ATTN_TQ = 512


def _attn(qk, vt, cum_rep, cum_row, bsz, seq, tq=ATTN_TQ):
    t = qk.shape[0]
    nq = seq // tq
    h = ATTN_HEADS
    cq = cum_row.reshape(bsz, h, nq, 1, tq)
    return pl.pallas_call(
        functools.partial(_attn_kernel, tq=tq),
        grid=(bsz, h, nq),
        in_specs=[pl.BlockSpec((tq, HEAD_DIM), lambda b, hh, i: (b * nq + i, hh)),
                  pl.BlockSpec((seq, HEAD_DIM), lambda b, hh, i: (b, h + hh)),
                  pl.BlockSpec((nq, HEAD_DIM, tq), lambda b, hh, i: (b, hh, 0)),
                  pl.BlockSpec((1, 1, 1, 1, tq), lambda b, hh, i: (b, hh, i, 0, 0)),
                  pl.BlockSpec((1, 1, seq, LANES), lambda b, hh, i: (b, hh, 0, 0))],
        out_specs=pl.BlockSpec((tq, HEAD_DIM), lambda b, hh, i: (b * nq + i, hh)),
        out_shape=jax.ShapeDtypeStruct((t, ATTN_WIDTH), F32),
        scratch_shapes=[pltpu.VMEM((1, tq), F32), pltpu.VMEM((1, tq), F32),
                        pltpu.VMEM((HEAD_DIM, tq), F32)],
        compiler_params=_params(("arbitrary", "arbitrary", "arbitrary")),
        name="attn",
    )(qk, qk, vt, cq, cum_rep)


def _lru_kernel(lx_ref, ly_ref, cw_ref, cb_ref, wa_ref, ba_ref, wx_ref, bx_ref, lam_ref,
                g_ref, o_ref, xbuf, hc, *, ts):
    @pl.when(pl.program_id(1) == 0)
    def _():
        xbuf[0:SUBLANES, :] = jnp.zeros((SUBLANES, LRU_WIDTH), F32)
        hc[...] = jnp.zeros_like(hc)

    xbuf[SUBLANES:, :] = lx_ref[...]
    xc = cb_ref[...]
    for k in range(CONV_WIDTH):
        off = SUBLANES - (CONV_WIDTH - 1) + k
        xc = xc + cw_ref[k:k + 1, :] * xbuf[off:off + ts, :]
    xbuf[0:SUBLANES, :] = xbuf[ts:ts + SUBLANES, :]

    xcb = xc.astype(BF16)
    ra, rx = [], []
    for n in range(LRU_BLOCKS):
        xs = xcb[:, n * LRU_BLOCK_DIM:(n + 1) * LRU_BLOCK_DIM]
        ra.append(jnp.dot(xs, wa_ref[n], preferred_element_type=F32))
        rx.append(jnp.dot(xs, wx_ref[n], preferred_element_type=F32))
    r = jax.nn.sigmoid(jnp.concatenate(ra, axis=1) + ba_ref[...])
    ig = jax.nn.sigmoid(jnp.concatenate(rx, axis=1) + bx_ref[...])
    log_a = (-LRU_C) * r * _softplus(-lam_ref[...])
    a = jnp.exp(log_a)
    b = jnp.sqrt(-jnp.tanh(log_a) * (a * a + 1.0)) * (ig * xc)

    rows = lax.broadcasted_iota(jnp.int32, a.shape, 0)
    s = 1
    while s < ts:
        keep = rows >= s
        b = jnp.where(keep, a * pltpu.roll(b, s, axis=0) + b, b)
        a = jnp.where(keep, a * pltpu.roll(a, s, axis=0), a)
        s *= 2
    hseq = b + a * hc[...]
    hc[...] = hseq[ts - 1:ts, :]

    lru = hseq * _gelu_tanh(ly_ref[...])
    o_ref[...] = (_rms(lru) * g_ref[...]).astype(o_ref.dtype)


def _lru(lxy, conv_w, conv_b, w_a, b_a, w_x, b_x, lam, g, bsz, seq, ts=256):
    t = lxy.shape[0]
    ns = seq // ts
    w = LRU_WIDTH
    vec = lambda: pl.BlockSpec((1, w), lambda b, s: (0, 0))
    blk = lambda: pl.BlockSpec((LRU_BLOCKS, LRU_BLOCK_DIM, LRU_BLOCK_DIM), lambda b, s: (0, 0, 0))
    return pl.pallas_call(
        functools.partial(_lru_kernel, ts=ts),
        grid=(bsz, ns),
        in_specs=[pl.BlockSpec((ts, w), lambda b, s: (b * ns + s, 0)),
                  pl.BlockSpec((ts, w), lambda b, s: (b * ns + s, 1)),
                  pl.BlockSpec((CONV_WIDTH, w), lambda b, s: (0, 0)),
                  vec(), blk(), vec(), blk(), vec(), vec(), vec()],
        out_specs=pl.BlockSpec((ts, w), lambda b, s: (b * ns + s, 0)),
        out_shape=jax.ShapeDtypeStruct((t, w), BF16),
        scratch_shapes=[pltpu.VMEM((ts + SUBLANES, w), F32), pltpu.VMEM((1, w), F32)],
        compiler_params=_params(("arbitrary", "arbitrary")),
        name="lru",
    )(lxy, lxy, conv_w, conv_b.reshape(1, w), w_a.astype(BF16), b_a.reshape(1, w),
      w_x.astype(BF16), b_x.reshape(1, w), lam.reshape(1, w), g.reshape(1, w))


def _outproj_kernel(at_ref, lr_ref, x_ref, ga_ref, woa_ref, wol_ref, g1_ref, sc_ref, sh_ref,
                    gf_ref, x1_ref, h2_ref, h2t_ref):
    an = (_rms(at_ref[...]) * ga_ref[...]).astype(BF16)
    mix = jnp.dot(an, woa_ref[...], preferred_element_type=F32)
    mix = mix + jnp.dot(lr_ref[...], wol_ref[...], preferred_element_type=F32)
    x1 = x_ref[...] + g1_ref[0] * mix
    x1_ref[...] = x1
    h2 = (_rms(x1) * gf_ref[...]) * (1.0 + sc_ref[0]) + sh_ref[0]
    h2_ref[...] = h2.astype(h2_ref.dtype)
    h2t_ref[...] = h2.T.astype(h2t_ref.dtype)


def _outproj(attn, lru_n, x2, g_attn, w_oa, w_ol, ga1, sc2, sh2, g_ffn, seq, tm=256):
    t, d = x2.shape
    nb = seq // tm
    aw = attn.shape[1]
    lw = lru_n.shape[1]
    mod = lambda: pl.BlockSpec((1, 1, d), lambda i: (i // nb, 0, 0))
    return pl.pallas_call(
        _outproj_kernel,
        grid=(t // tm,),
        in_specs=[pl.BlockSpec((tm, aw), lambda i: (i, 0)),
                  pl.BlockSpec((tm, lw), lambda i: (i, 0)),
                  pl.BlockSpec((tm, d), lambda i: (i, 0)),
                  pl.BlockSpec((1, aw), lambda i: (0, 0)),
                  pl.BlockSpec((aw, d), lambda i: (0, 0)),
                  pl.BlockSpec((lw, d), lambda i: (0, 0)),
                  mod(), mod(), mod(),
                  pl.BlockSpec((1, d), lambda i: (0, 0))],
        out_specs=[pl.BlockSpec((tm, d), lambda i: (i, 0)),
                   pl.BlockSpec((tm, d), lambda i: (i, 0)),
                   pl.BlockSpec((d, tm), lambda i: (0, i))],
        out_shape=[jax.ShapeDtypeStruct((t, d), F32), jax.ShapeDtypeStruct((t, d), BF16),
                   jax.ShapeDtypeStruct((d, t), BF16)],
        compiler_params=_params(("arbitrary",)),
        name="outproj",
    )(attn, lru_n, x2, g_attn.reshape(1, aw), w_oa, w_ol, ga1, sc2, sh2, g_ffn.reshape(1, d))


def _sort16_pairs():
    pairs = []

    def merge(lo, hi, r):
        step = r * 2
        if step < hi - lo:
            merge(lo, hi, step)
            merge(lo + r, hi, step)
            for i in range(lo + r, hi - r, step):
                pairs.append((i, i + r))
        else:
            pairs.append((lo, lo + r))

    def sort(lo, hi):
        if hi - lo >= 1:
            mid = lo + (hi - lo) // 2
            sort(lo, mid)
            sort(mid + 1, hi)
            merge(lo, hi, 1)

    sort(0, PEER_TOPK - 1)
    return pairs


_SORT16 = _sort16_pairs()


def _cex(v, i, j):
    hi, lo = jnp.maximum(v[i], v[j]), jnp.minimum(v[i], v[j])
    v[i], v[j] = hi, lo


def _sort16(v):
    v = list(v)
    for i, j in _SORT16:
        _cex(v, i, j)
    return v


def _bitonic_merge16(v):
    v = list(v)
    d = PEER_TOPK // 2
    while d >= 1:
        for i in range(PEER_TOPK):
            if i & d == 0:
                _cex(v, i, i + d)
        d //= 2
    return v


def _top16_of_union(a, b):
    return [jnp.maximum(a[i], b[PEER_TOPK - 1 - i]) for i in range(PEER_TOPK)]


def _top16_sorted(s):
    v = _sort16([s[i * SUBLANES:(i + 1) * SUBLANES, :] for i in range(PEER_TOPK)])
    for shift in (4, 2, 1):
        other = [pltpu.roll(x, shift, axis=0) for x in v]
        v = _bitonic_merge16(_top16_of_union(v, other))
    return v


def _threshold_stats(v1, v2):
    k = PEER_TOPK
    cells = {(i, j): v1[i] + v2[j] for i in range(k) for j in range(k) if (i + 1) * (j + 1) <= k}
    row0 = [cells[0, j] for j in range(k)]
    col0 = [cells[i, 0] for i in range(1, k)]
    rest = ([(1, j) for j in range(1, 8)] + [(i, 1) for i in range(2, 8)]
            + [(2, j) for j in range(2, 5)] + [(i, 2) for i in range(3, 5)] + [(3, 3)])
    rest = [cells[ij] for ij in rest]
    ab = [row0[0]] + [jnp.maximum(row0[i], col0[k - 1 - i]) for i in range(1, k)]
    ab = _bitonic_merge16(ab)
    c1 = _sort16(rest[:k])
    t = _bitonic_merge16(_top16_of_union(ab, c1))
    y = list(rest[k:])
    _cex(y, 0, 1), _cex(y, 1, 2), _cex(y, 0, 1)
    thr = jnp.minimum(jnp.minimum(t[12], jnp.maximum(t[13], y[2])),
                      jnp.minimum(jnp.maximum(t[14], y[1]), jnp.maximum(t[15], y[0])))
    top = row0[0]
    z = jnp.zeros_like(top)
    for c in cells.values():
        z = z + jnp.where(c >= thr, jnp.exp(c - top), 0.0)
    t = []
    for j in range(k):
        tj = jnp.full_like(top, jnp.inf)
        for i in range(k // (j + 1)):
            tj = jnp.where(cells[i, j] >= thr, v1[i], tj)
        t.append(tj)
    return z, t


def _topk_kernel(qp_ref, k1_ref, k2_ref, c1_ref, a1_ref, s2_ref, b2_ref):
    reps = PEER_KEYS // SUBLANES

    def head(h):
        q = qp_ref[h]
        s1 = lax.dot_general(k1_ref[h], q[:, :PEER_HALF], _NT, preferred_element_type=F32)
        s2 = lax.dot_general(k2_ref[h], q[:, PEER_HALF:], _NT, preferred_element_type=F32)
        v1 = _top16_sorted(s1)
        v2 = _top16_sorted(s2)
        z, t = _threshold_stats(v1, v2)
        c1 = jnp.full_like(s1, jnp.inf)
        for tj, v in zip(t, v2):
            c1 = jnp.where(s1 >= jnp.tile(tj, (reps, 1)), jnp.tile(v, (reps, 1)), c1)
        c1_ref[0, h] = c1
        s2_ref[0, h] = s2
        a1_ref[0, h] = jnp.exp(s1 - v1[0][0:1, :]) / z[0:1, :]
        b2_ref[0, h] = jnp.exp(s2 - v2[0][0:1, :])

    def two_heads(i, carry):
        head(2 * i)
        head(2 * i + 1)
        return carry

    lax.fori_loop(0, PEER_HEADS // 2, two_heads, 0)


def _topk(qp3, k1, k2):
    nh, t, qd = qp3.shape
    nc = t // LANES
    big = lambda: pl.BlockSpec((1, nh, PEER_KEYS, LANES), lambda i: (i, 0, 0, 0))
    keys = lambda: pl.BlockSpec((nh, PEER_KEYS, PEER_HALF), lambda i: (0, 0, 0))
    sds = jax.ShapeDtypeStruct((nc, nh, PEER_KEYS, LANES), F32)
    return pl.pallas_call(
        _topk_kernel,
        grid=(nc,),
        in_specs=[pl.BlockSpec((nh, LANES, qd), lambda i: (0, i, 0)), keys(), keys()],
        out_specs=[big(), big(), big(), big()],
        out_shape=[sds, sds, sds, sds],
        compiler_params=_params(("arbitrary",)),
        name="topk",
    )(qp3, k1, k2)


PEER_TILE = SUBLANES * PEER_KEYS


def _peer_kernel(ht_ref, u_ref, vt_ref, c1_ref, a1_ref, s2_ref, b2_ref, x1_ref, g2_ref, gf_ref,
                 o_ref, acc, st, act, *, nc):
    j = pl.program_id(1)

    @pl.when(j == 0)
    def _():
        acc[...] = jnp.zeros_like(acc)

    @pl.when(j >= 0)
    def _():
        st[...] = jnp.dot(u_ref[...], ht_ref[...], preferred_element_type=F32)

    @pl.when(j >= -1)
    def _():
        for c in range(nc):
            cs = slice(c * LANES, (c + 1) * LANES)
            for r in range(SUBLANES):
                w = jnp.zeros((PEER_KEYS, LANES), F32)
                for h in range(PEER_HEADS):
                    sel = jnp.where(s2_ref[c, h] >= c1_ref[c, h, r:r + 1, :], b2_ref[c, h], 0.0)
                    w = w + sel * a1_ref[c, h, r:r + 1, :]
                rows = slice(r * PEER_KEYS, (r + 1) * PEER_KEYS)
                act[rows, cs] = (_gelu_tanh(st[rows, cs]) * w).astype(BF16)

    @pl.when(j >= -2)
    def _():
        acc[...] += jnp.dot(vt_ref[0], act[...], preferred_element_type=F32)

    @pl.when(j == pl.num_programs(1) - 1)
    def _():
        o_ref[...] = _rms(x1_ref[...] + g2_ref[0] * acc[...].T) * gf_ref[...]


def _peer(h2t, u_b, v_tiles, c1, a1, s2, b2, x1, ga2, g_final, seq, tm=512):
    d, t = h2t.shape
    te = PEER_TILE
    ne = u_b.shape[0] // te
    nh = c1.shape[1]
    nc = tm // LANES
    nb = seq // tm
    rows = lambda: pl.BlockSpec((nc, nh, SUBLANES, LANES), lambda i, j: (i, 0, j, 0))
    full = lambda: pl.BlockSpec((nc, nh, PEER_KEYS, LANES), lambda i, j: (i, 0, 0, 0),
                                pipeline_mode=pl.Buffered(1))
    return pl.pallas_call(
        functools.partial(_peer_kernel, nc=nc),
        grid=(t // tm, ne),
        in_specs=[pl.BlockSpec((d, tm), lambda i, j: (0, i)),
                  pl.BlockSpec((te, d), lambda i, j: (j, 0)),
                  pl.BlockSpec((1, d, te), lambda i, j: (j, 0, 0)),
                  rows(), rows(), full(), full(),
                  pl.BlockSpec((tm, d), lambda i, j: (i, 0)),
                  pl.BlockSpec((1, 1, d), lambda i, j: (i // nb, 0, 0)),
                  pl.BlockSpec((1, d), lambda i, j: (0, 0))],
        out_specs=pl.BlockSpec((tm, d), lambda i, j: (i, 0)),
        out_shape=jax.ShapeDtypeStruct((t, d), F32),
        scratch_shapes=[pltpu.VMEM((d, tm), F32), pltpu.VMEM((te, tm), F32),
                        pltpu.VMEM((te, tm), BF16)],
        compiler_params=_params(("arbitrary", "arbitrary")),
        name="peer",
    )(h2t, u_b, v_tiles, c1, a1, s2, b2, x1, ga2, g_final.reshape(1, d))


def kernel(x, c, w_ada, b_ada, g_mix, w_in, b_f, conv_w, conv_b, lru_w_a, lru_b_a, lru_w_x,
           lru_b_x, lru_lambda, g_attn_out, g_lru_out, w_out, g_ffn, peer_w_q, peer_k1,
           peer_k2, peer_u, peer_v, g_final):
    bsz, seq, d = x.shape
    depth = w_ada.shape[0]
    assert depth == 1, "the final residual + norm is fused into the PEER kernel of a single layer"
    x2 = x.reshape(bsz * seq, d)
    off_f = 3 * ATTN_WIDTH
    off_l = off_f + ATTN_HEADS

    for l in range(depth):
        mod = _ada(c, w_ada[l], b_ada[l])
        sh1, sc1, ga1, sh2, sc2, ga2 = [m.reshape(bsz, 1, d) for m in jnp.split(mod, N_ADA, axis=-1)]

        h = _modnorm(x2, g_mix[l], sc1, sh1, seq)
        w_l = w_in[l]
        q_scale = jnp.where(jnp.arange(2 * ATTN_WIDTH) < ATTN_WIDTH, LOG2E * HEAD_DIM ** -0.5, 1.0)
        qk = _matmul_colscale(h, w_l[:, :2 * ATTN_WIDTH].astype(BF16), q_scale.astype(F32), BF16,
                              name="proj_qk")
        vt = _matmul_t(h, w_l[:, 2 * ATTN_WIDTH:off_f].T.astype(BF16), BF16, tm=ATTN_TQ,
                       name="proj_vt")
        w_f = jnp.pad(w_l[:, off_f:off_l], ((0, 0), (0, LANES - ATTN_HEADS))).astype(BF16)
        fl = _matmul(h, w_f, F32, name="proj_f")
        lxy = _matmul(h, w_l[:, off_l:].astype(BF16), F32, name="proj_lru")

        bf = jnp.pad(b_f[l], (0, LANES - ATTN_HEADS)).reshape(1, LANES)
        cum_rep, cum_row = _cum(fl, bf, bsz, seq)
        attn = _attn(qk, vt, cum_rep, cum_row, bsz, seq)
        lru_n = _lru(lxy, conv_w[l], conv_b[l], lru_w_a[l], lru_b_a[l], lru_w_x[l], lru_b_x[l],
                     lru_lambda[l], g_lru_out[l], bsz, seq)

        w_o = w_out[l].astype(BF16)
        x1, h2, h2t = _outproj(attn, lru_n, x2, g_attn_out[l], w_o[:ATTN_WIDTH],
                               w_o[ATTN_WIDTH:], ga1, sc2, sh2, g_ffn[l], seq)

        qp3 = _matmul_heads(h2, peer_w_q[l].astype(BF16), PEER_HEADS, BF16, name="peer_q")
        c1, a1, s2, b2 = _topk(qp3, peer_k1[l].astype(BF16), peer_k2[l].astype(BF16))
        v_tiles = peer_v[l].reshape(-1, PEER_TILE, d).transpose(0, 2, 1).astype(BF16)
        out = _peer(h2t, peer_u[l].astype(BF16), v_tiles, c1, a1, s2, b2, x1, ga2, g_final, seq)
    return out.reshape(bsz, seq, d)
```

```python
import functools

import jax
import jax.numpy as jnp
from jax import lax
from jax.experimental import pallas as pl
from jax.experimental.pallas import tpu as pltpu

F32 = jnp.float32
BF16 = jnp.bfloat16

ATTN_HEADS = 8
HEAD_DIM = 128
ATTN_WIDTH = ATTN_HEADS * HEAD_DIM
LRU_BLOCKS = 8
LRU_BLOCK_DIM = 128
LRU_WIDTH = LRU_BLOCKS * LRU_BLOCK_DIM
CONV_WIDTH = 4
LRU_C = 8.0
PEER_HEADS = 8
PEER_KEYS = 128
PEER_HALF = 128
PEER_TOPK = 16
N_ADA = 6
EPS = 1e-6
LANES = 128
SUBLANES = 8
NEG_BIG = -1e30
LOG2E = 1.4426950408889634
VMEM_LIMIT = 56 * 1024 * 1024

_NT = (((1,), (1,)), ((), ()))


def _params(sem, vmem=VMEM_LIMIT):
    return pltpu.CompilerParams(dimension_semantics=sem, vmem_limit_bytes=vmem)


def _gelu_tanh(x):
    return 0.5 * x * (1.0 + jnp.tanh(0.7978845608028654 * (x + 0.044715 * (x * x * x))))


def _softplus(z):
    return jnp.maximum(z, 0.0) + jnp.log1p(jnp.exp(-jnp.abs(z)))


def _rms(x):
    return x * lax.rsqrt(jnp.mean(x * x, axis=-1, keepdims=True) + EPS)


def _ada_kernel(c_ref, w_ref, b_ref, o_ref):
    @pl.when(pl.program_id(0) == 0)
    def _():
        o_ref[...] = jnp.broadcast_to(b_ref[...], o_ref.shape)

    c = c_ref[...]
    sc = c * jax.nn.sigmoid(c)
    o_ref[...] += jnp.dot(sc.astype(BF16), w_ref[...].astype(BF16), preferred_element_type=F32)


def _ada(c, w, b, tk=256):
    bsz, d = c.shape
    n = w.shape[1]
    return pl.pallas_call(
        _ada_kernel,
        grid=(d // tk,),
        in_specs=[pl.BlockSpec((bsz, tk), lambda k: (0, k)),
                  pl.BlockSpec((tk, n), lambda k: (k, 0)),
                  pl.BlockSpec((1, n), lambda k: (0, 0))],
        out_specs=pl.BlockSpec((bsz, n), lambda k: (0, 0)),
        out_shape=jax.ShapeDtypeStruct((bsz, n), F32),
        compiler_params=_params(("arbitrary",)),
        name="ada",
    )(c, w, b.reshape(1, n))


def _modnorm_kernel(x_ref, g_ref, sc_ref, sh_ref, o_ref):
    y = _rms(x_ref[...]) * g_ref[...]
    o_ref[...] = (y * (1.0 + sc_ref[0]) + sh_ref[0]).astype(o_ref.dtype)


def _modnorm(x2, g, sc, sh, seq, tm=512):
    t, d = x2.shape
    nb = seq // tm
    return pl.pallas_call(
        _modnorm_kernel,
        grid=(t // tm,),
        in_specs=[pl.BlockSpec((tm, d), lambda i: (i, 0)),
                  pl.BlockSpec((1, d), lambda i: (0, 0)),
                  pl.BlockSpec((1, 1, d), lambda i: (i // nb, 0, 0)),
                  pl.BlockSpec((1, 1, d), lambda i: (i // nb, 0, 0))],
        out_specs=pl.BlockSpec((tm, d), lambda i: (i, 0)),
        out_shape=jax.ShapeDtypeStruct((t, d), BF16),
        compiler_params=_params(("arbitrary",)),
        name="modnorm",
    )(x2, g.reshape(1, d), sc, sh)


def _matmul_kernel(a_ref, w_ref, o_ref):
    o_ref[...] = jnp.dot(a_ref[...], w_ref[...],
                         preferred_element_type=F32).astype(o_ref.dtype)


def _matmul(a, w, out_dtype, tm=512, tn=1024, name="matmul"):
    t, k = a.shape
    n = w.shape[1]
    tn = min(tn, n)
    return pl.pallas_call(
        _matmul_kernel,
        grid=(n // tn, t // tm),
        in_specs=[pl.BlockSpec((tm, k), lambda j, i: (i, 0)),
                  pl.BlockSpec((k, tn), lambda j, i: (0, j))],
        out_specs=pl.BlockSpec((tm, tn), lambda j, i: (i, j)),
        out_shape=jax.ShapeDtypeStruct((t, n), out_dtype),
        compiler_params=_params(("arbitrary", "arbitrary")),
        name=name,
    )(a, w)


def _matmul_colscale_kernel(a_ref, w_ref, cs_ref, o_ref):
    acc = jnp.dot(a_ref[...], w_ref[...], preferred_element_type=F32)
    o_ref[...] = (acc * cs_ref[...]).astype(o_ref.dtype)


def _matmul_colscale(a, w, col_scale, out_dtype, tm=512, tn=1024, name="matmul_colscale"):
    t, k = a.shape
    n = w.shape[1]
    return pl.pallas_call(
        _matmul_colscale_kernel,
        grid=(n // tn, t // tm),
        in_specs=[pl.BlockSpec((tm, k), lambda j, i: (i, 0)),
                  pl.BlockSpec((k, tn), lambda j, i: (0, j)),
                  pl.BlockSpec((1, tn), lambda j, i: (0, j))],
        out_specs=pl.BlockSpec((tm, tn), lambda j, i: (i, j)),
        out_shape=jax.ShapeDtypeStruct((t, n), out_dtype),
        compiler_params=_params(("arbitrary", "arbitrary")),
        name=name,
    )(a, w, col_scale.reshape(1, n))


def _matmul_t_kernel(a_ref, wt_ref, o_ref):
    o_ref[0] = lax.dot_general(wt_ref[...], a_ref[...], _NT,
                               preferred_element_type=F32).astype(o_ref.dtype)


def _matmul_t(a, wt, out_dtype, tm=512, tn=1024, name="matmul_t"):
    t, k = a.shape
    n = wt.shape[0]
    tn = min(tn, n)
    return pl.pallas_call(
        _matmul_t_kernel,
        grid=(n // tn, t // tm),
        in_specs=[pl.BlockSpec((tm, k), lambda j, i: (i, 0)),
                  pl.BlockSpec((tn, k), lambda j, i: (j, 0))],
        out_specs=pl.BlockSpec((1, tn, tm), lambda j, i: (i, j, 0)),
        out_shape=jax.ShapeDtypeStruct((t // tm, n, tm), out_dtype),
        compiler_params=_params(("arbitrary", "arbitrary")),
        name=name,
    )(a, wt)


def _cum_kernel(fl_ref, bf_ref, rep_ref, row_ref, carry_ref, *, ts):
    @pl.when(pl.program_id(1) == 0)
    def _():
        carry_ref[...] = jnp.zeros_like(carry_ref)

    lf = -_softplus(-(fl_ref[...] + bf_ref[...]))
    rows = lax.broadcasted_iota(jnp.int32, lf.shape, 0)
    s = 1
    while s < ts:
        lf = lf + jnp.where(rows >= s, pltpu.roll(lf, s, axis=0), 0.0)
        s *= 2
    cum = lf + carry_ref[...]
    carry_ref[...] = cum[ts - 1:ts, :]
    cum2 = cum * LOG2E
    for h in range(ATTN_HEADS):
        rep_ref[0, h] = jnp.broadcast_to(cum2[:, h:h + 1], (ts, LANES))
    row_ref[0] = cum2.T[0:ATTN_HEADS, :]


def _cum(fl, bf, bsz, seq, ts=512):
    ns = seq // ts
    return pl.pallas_call(
        functools.partial(_cum_kernel, ts=ts),
        grid=(bsz, ns),
        in_specs=[pl.BlockSpec((ts, LANES), lambda b, s: (b * ns + s, 0)),
                  pl.BlockSpec((1, LANES), lambda b, s: (0, 0))],
        out_specs=[pl.BlockSpec((1, ATTN_HEADS, ts, LANES), lambda b, s: (b, 0, s, 0)),
                   pl.BlockSpec((1, ATTN_HEADS, ts), lambda b, s: (b, 0, s))],
        out_shape=[jax.ShapeDtypeStruct((bsz, ATTN_HEADS, seq, LANES), F32),
                   jax.ShapeDtypeStruct((bsz, ATTN_HEADS, seq), F32)],
        scratch_shapes=[pltpu.VMEM((1, LANES), F32)],
        compiler_params=_params(("arbitrary", "arbitrary")),
        name="cum",
    )(fl, bf)


def _attn_kernel(q_ref, k_ref, vt_ref, cq_ref, ck_ref, o_ref, m_ref, l_ref, acc_ref, *, tq):
    qi = pl.program_id(2)
    q = q_ref[...]
    cq = cq_ref[0, 0, 0]
    m_ref[...] = jnp.full_like(m_ref, NEG_BIG)
    l_ref[...] = jnp.zeros_like(l_ref)
    acc_ref[...] = jnp.zeros_like(acc_ref)
    reps = tq // LANES

    def step(j, nblk, masked):
        tk = nblk * tq
        ks = pl.multiple_of(j * tq, tq)
        k = k_ref[pl.ds(ks, tk), :]
        ck = ck_ref[0, 0, pl.ds(ks, tk), :]
        s = lax.dot_general(k, q, _NT, preferred_element_type=F32)
        s = s + (cq - jnp.concatenate([ck] * reps, axis=1))
        if masked:
            key = lax.broadcasted_iota(jnp.int32, (tk, tq), 0)
            qry = lax.broadcasted_iota(jnp.int32, (tk, tq), 1)
            s = jnp.where(key <= qry, s, NEG_BIG)
        m_prev = m_ref[...]
        m_new = jnp.maximum(m_prev, jnp.max(s, axis=0, keepdims=True))
        alpha = jnp.exp2(m_prev - m_new)
        p = jnp.exp2(s - m_new)
        l_ref[...] = alpha * l_ref[...] + jnp.sum(p, axis=0, keepdims=True)
        pb = p.astype(BF16)
        pv = jnp.dot(vt_ref[j], pb[0:tq], preferred_element_type=F32)
        for b in range(1, nblk):
            pv = pv + jnp.dot(vt_ref[j + b], pb[b * tq:(b + 1) * tq], preferred_element_type=F32)
        acc_ref[...] = alpha * acc_ref[...] + pv
        m_ref[...] = m_new

    def quad(i, carry):
        step(4 * i, 4, False)
        return carry

    lax.fori_loop(0, qi // 4, quad, 0)
    rest = 4 * (qi // 4)

    @pl.when(qi % 4 >= 2)
    def _():
        step(rest, 2, False)

    @pl.when(qi % 2 == 1)
    def _():
        step(qi - 1, 1, False)

    step(qi, 1, True)
    o_ref[...] = (acc_ref[...] / l_ref[...]).T


ATTN_TQ = 512


def _attn(qk, vt, cum_rep, cum_row, bsz, seq, tq=ATTN_TQ):
    t = qk.shape[0]
    nq = seq // tq
    h = ATTN_HEADS
    cq = cum_row.reshape(bsz, h, nq, 1, tq)
    return pl.pallas_call(
        functools.partial(_attn_kernel, tq=tq),
        grid=(bsz, h, nq),
        in_specs=[pl.BlockSpec((tq, HEAD_DIM), lambda b, hh, i: (b * nq + i, hh)),
                  pl.BlockSpec((seq, HEAD_DIM), lambda b, hh, i: (b, h + hh)),
                  pl.BlockSpec((nq, HEAD_DIM, tq), lambda b, hh, i: (b, hh, 0)),
                  pl.BlockSpec((1, 1, 1, 1, tq), lambda b, hh, i: (b, hh, i, 0, 0)),
                  pl.BlockSpec((1, 1, seq, LANES), lambda b, hh, i: (b, hh, 0, 0))],
        out_specs=pl.BlockSpec((tq, HEAD_DIM), lambda b, hh, i: (b * nq + i, hh)),
        out_shape=jax.ShapeDtypeStruct((t, ATTN_WIDTH), F32),
        scratch_shapes=[pltpu.VMEM((1, tq), F32), pltpu.VMEM((1, tq), F32),
                        pltpu.VMEM((HEAD_DIM, tq), F32)],
        compiler_params=_params(("arbitrary", "arbitrary", "arbitrary")),
        name="attn",
    )(qk, qk, vt, cq, cum_rep)


def _lru_kernel(lx_ref, ly_ref, cw_ref, cb_ref, wa_ref, ba_ref, wx_ref, bx_ref, lam_ref,
                g_ref, o_ref, xbuf, hc, *, ts):
    @pl.when(pl.program_id(1) == 0)
    def _():
        xbuf[0:SUBLANES, :] = jnp.zeros((SUBLANES, LRU_WIDTH), F32)
        hc[...] = jnp.zeros_like(hc)

    xbuf[SUBLANES:, :] = lx_ref[...]
    xc = cb_ref[...]
    for k in range(CONV_WIDTH):
        off = SUBLANES - (CONV_WIDTH - 1) + k
        xc = xc + cw_ref[k:k + 1, :] * xbuf[off:off + ts, :]
    xbuf[0:SUBLANES, :] = xbuf[ts:ts + SUBLANES, :]

    xcb = xc.astype(BF16)
    ra, rx = [], []
    for n in range(LRU_BLOCKS):
        xs = xcb[:, n * LRU_BLOCK_DIM:(n + 1) * LRU_BLOCK_DIM]
        ra.append(jnp.dot(xs, wa_ref[n], preferred_element_type=F32))
        rx.append(jnp.dot(xs, wx_ref[n], preferred_element_type=F32))
    r = jax.nn.sigmoid(jnp.concatenate(ra, axis=1) + ba_ref[...])
    ig = jax.nn.sigmoid(jnp.concatenate(rx, axis=1) + bx_ref[...])
    log_a = (-LRU_C) * r * _softplus(-lam_ref[...])
    a = jnp.exp(log_a)
    b = jnp.sqrt(-jnp.tanh(log_a) * (a * a + 1.0)) * (ig * xc)

    rows = lax.broadcasted_iota(jnp.int32, a.shape, 0)
    s = 1
    while s < ts:
        keep = rows >= s
        b = jnp.where(keep, a * pltpu.roll(b, s, axis=0) + b, b)
        a = jnp.where(keep, a * pltpu.roll(a, s, axis=0), a)
        s *= 2
    hseq = b + a * hc[...]
    hc[...] = hseq[ts - 1:ts, :]

    lru = hseq * _gelu_tanh(ly_ref[...])
    o_ref[...] = (_rms(lru) * g_ref[...]).astype(o_ref.dtype)


def _lru(lxy, conv_w, conv_b, w_a, b_a, w_x, b_x, lam, g, bsz, seq, ts=256):
    t = lxy.shape[0]
    ns = seq // ts
    w = LRU_WIDTH
    vec = lambda: pl.BlockSpec((1, w), lambda b, s: (0, 0))
    blk = lambda: pl.BlockSpec((LRU_BLOCKS, LRU_BLOCK_DIM, LRU_BLOCK_DIM), lambda b, s: (0, 0, 0))
    return pl.pallas_call(
        functools.partial(_lru_kernel, ts=ts),
        grid=(bsz, ns),
        in_specs=[pl.BlockSpec((ts, w), lambda b, s: (b * ns + s, 0)),
                  pl.BlockSpec((ts, w), lambda b, s: (b * ns + s, 1)),
                  pl.BlockSpec((CONV_WIDTH, w), lambda b, s: (0, 0)),
                  vec(), blk(), vec(), blk(), vec(), vec(), vec()],
        out_specs=pl.BlockSpec((ts, w), lambda b, s: (b * ns + s, 0)),
        out_shape=jax.ShapeDtypeStruct((t, w), BF16),
        scratch_shapes=[pltpu.VMEM((ts + SUBLANES, w), F32), pltpu.VMEM((1, w), F32)],
        compiler_params=_params(("arbitrary", "arbitrary")),
        name="lru",
    )(lxy, lxy, conv_w, conv_b.reshape(1, w), w_a.astype(BF16), b_a.reshape(1, w),
      w_x.astype(BF16), b_x.reshape(1, w), lam.reshape(1, w), g.reshape(1, w))


def _outproj_kernel(at_ref, lr_ref, x_ref, ga_ref, woa_ref, wol_ref, g1_ref, sc_ref, sh_ref,
                    gf_ref, wq_ref, x1_ref, h2t_ref, qp_ref):
    an = (_rms(at_ref[...]) * ga_ref[...]).astype(BF16)
    mix = jnp.dot(an, woa_ref[...], preferred_element_type=F32)
    mix = mix + jnp.dot(lr_ref[...], wol_ref[...], preferred_element_type=F32)
    x1 = x_ref[...] + g1_ref[0] * mix
    x1_ref[...] = x1
    h2 = (_rms(x1) * gf_ref[...]) * (1.0 + sc_ref[0]) + sh_ref[0]
    h2t_ref[...] = h2.T.astype(h2t_ref.dtype)
    qp = jnp.dot(h2.astype(BF16), wq_ref[...], preferred_element_type=F32).astype(qp_ref.dtype)
    width = qp_ref.shape[2]
    for h in range(qp_ref.shape[0]):
        qp_ref[h] = qp[:, h * width:(h + 1) * width]


def _outproj(attn, lru_n, x2, g_attn, w_oa, w_ol, ga1, sc2, sh2, g_ffn, w_q, nh, seq, tm=256):
    t, d = x2.shape
    nb = seq // tm
    aw = attn.shape[1]
    lw = lru_n.shape[1]
    nq = w_q.shape[1]
    mod = lambda: pl.BlockSpec((1, 1, d), lambda i: (i // nb, 0, 0))
    return pl.pallas_call(
        _outproj_kernel,
        grid=(t // tm,),
        in_specs=[pl.BlockSpec((tm, aw), lambda i: (i, 0)),
                  pl.BlockSpec((tm, lw), lambda i: (i, 0)),
                  pl.BlockSpec((tm, d), lambda i: (i, 0)),
                  pl.BlockSpec((1, aw), lambda i: (0, 0)),
                  pl.BlockSpec((aw, d), lambda i: (0, 0)),
                  pl.BlockSpec((lw, d), lambda i: (0, 0)),
                  mod(), mod(), mod(),
                  pl.BlockSpec((1, d), lambda i: (0, 0)),
                  pl.BlockSpec((d, nq), lambda i: (0, 0))],
        out_specs=[pl.BlockSpec((tm, d), lambda i: (i, 0)),
                   pl.BlockSpec((d, tm), lambda i: (0, i)),
                   pl.BlockSpec((nh, tm, nq // nh), lambda i: (0, i, 0))],
        out_shape=[jax.ShapeDtypeStruct((t, d), F32), jax.ShapeDtypeStruct((d, t), BF16),
                   jax.ShapeDtypeStruct((nh, t, nq // nh), BF16)],
        compiler_params=_params(("arbitrary",)),
        name="outproj",
    )(attn, lru_n, x2, g_attn.reshape(1, aw), w_oa, w_ol, ga1, sc2, sh2, g_ffn.reshape(1, d), w_q)


def _sort16_pairs():
    pairs = []

    def merge(lo, hi, r):
        step = r * 2
        if step < hi - lo:
            merge(lo, hi, step)
            merge(lo + r, hi, step)
            for i in range(lo + r, hi - r, step):
                pairs.append((i, i + r))
        else:
            pairs.append((lo, lo + r))

    def sort(lo, hi):
        if hi - lo >= 1:
            mid = lo + (hi - lo) // 2
            sort(lo, mid)
            sort(mid + 1, hi)
            merge(lo, hi, 1)

    sort(0, PEER_TOPK - 1)
    return pairs


_SORT16 = _sort16_pairs()


def _cex(v, i, j):
    hi, lo = jnp.maximum(v[i], v[j]), jnp.minimum(v[i], v[j])
    v[i], v[j] = hi, lo


def _sort16(v):
    v = list(v)
    for i, j in _SORT16:
        _cex(v, i, j)
    return v


def _bitonic_merge16(v):
    v = list(v)
    d = PEER_TOPK // 2
    while d >= 1:
        for i in range(PEER_TOPK):
            if i & d == 0:
                _cex(v, i, i + d)
        d //= 2
    return v


def _top16_of_union(a, b):
    return [jnp.maximum(a[i], b[PEER_TOPK - 1 - i]) for i in range(PEER_TOPK)]


def _top16_sorted(s):
    v = _sort16([s[i * SUBLANES:(i + 1) * SUBLANES, :] for i in range(PEER_TOPK)])
    for shift in (4, 2, 1):
        other = [pltpu.roll(x, shift, axis=0) for x in v]
        v = _bitonic_merge16(_top16_of_union(v, other))
    return v


def _threshold_stats(v1, v2):
    k = PEER_TOPK
    cells = {(i, j): v1[i] + v2[j] for i in range(k) for j in range(k) if (i + 1) * (j + 1) <= k}
    row0 = [cells[0, j] for j in range(k)]
    col0 = [cells[i, 0] for i in range(1, k)]
    rest = ([(1, j) for j in range(1, 8)] + [(i, 1) for i in range(2, 8)]
            + [(2, j) for j in range(2, 5)] + [(i, 2) for i in range(3, 5)] + [(3, 3)])
    rest = [cells[ij] for ij in rest]
    ab = [row0[0]] + [jnp.maximum(row0[i], col0[k - 1 - i]) for i in range(1, k)]
    ab = _bitonic_merge16(ab)
    c1 = _sort16(rest[:k])
    t = _bitonic_merge16(_top16_of_union(ab, c1))
    y = list(rest[k:])
    _cex(y, 0, 1), _cex(y, 1, 2), _cex(y, 0, 1)
    thr = jnp.minimum(jnp.minimum(t[12], jnp.maximum(t[13], y[2])),
                      jnp.minimum(jnp.maximum(t[14], y[1]), jnp.maximum(t[15], y[0])))
    top = row0[0]
    z = jnp.zeros_like(top)
    for c in cells.values():
        z = z + jnp.where(c >= thr, jnp.exp(c - top), 0.0)
    t = []
    for j in range(k):
        tj = jnp.full_like(top, jnp.inf)
        for i in range(k // (j + 1)):
            tj = jnp.where(cells[i, j] >= thr, v1[i], tj)
        t.append(tj)
    return z, t


def _topk_kernel(qp_ref, k1_ref, k2_ref, c1_ref, a1_ref, s2_ref, b2_ref):
    reps = PEER_KEYS // SUBLANES

    def head(h):
        q = qp_ref[h]
        s1 = lax.dot_general(k1_ref[h], q[:, :PEER_HALF], _NT, preferred_element_type=F32)
        s2 = lax.dot_general(k2_ref[h], q[:, PEER_HALF:], _NT, preferred_element_type=F32)
        v1 = _top16_sorted(s1)
        v2 = _top16_sorted(s2)
        z, t = _threshold_stats(v1, v2)
        c1 = jnp.full_like(s1, jnp.inf)
        for tj, v in zip(t, v2):
            c1 = jnp.where(s1 >= jnp.tile(tj, (reps, 1)), jnp.tile(v, (reps, 1)), c1)
        c1_ref[0, h] = c1
        s2_ref[0, h] = s2
        a1_ref[0, h] = jnp.exp(s1 - v1[0][0:1, :]) / z[0:1, :]
        b2_ref[0, h] = jnp.exp(s2 - v2[0][0:1, :])

    def two_heads(i, carry):
        head(2 * i)
        head(2 * i + 1)
        return carry

    lax.fori_loop(0, PEER_HEADS // 2, two_heads, 0)


def _topk(qp3, k1, k2):
    nh, t, qd = qp3.shape
    nc = t // LANES
    big = lambda: pl.BlockSpec((1, nh, PEER_KEYS, LANES), lambda i: (i, 0, 0, 0))
    keys = lambda: pl.BlockSpec((nh, PEER_KEYS, PEER_HALF), lambda i: (0, 0, 0))
    sds = jax.ShapeDtypeStruct((nc, nh, PEER_KEYS, LANES), F32)
    return pl.pallas_call(
        _topk_kernel,
        grid=(nc,),
        in_specs=[pl.BlockSpec((nh, LANES, qd), lambda i: (0, i, 0)), keys(), keys()],
        out_specs=[big(), big(), big(), big()],
        out_shape=[sds, sds, sds, sds],
        compiler_params=_params(("arbitrary",)),
        name="topk",
    )(qp3, k1, k2)


PEER_TILE = SUBLANES * PEER_KEYS


def _peer_kernel(ht_ref, u_ref, vt_ref, c1_ref, a1_ref, s2_ref, b2_ref, x1_ref, g2_ref, gf_ref,
                 o_ref, acc, st, act, *, nc):
    j = pl.program_id(1)

    @pl.when(j == 0)
    def _():
        acc[...] = jnp.zeros_like(acc)

    @pl.when(j >= 0)
    def _():
        st[...] = jnp.dot(u_ref[...], ht_ref[...], preferred_element_type=F32)

    @pl.when(j >= -1)
    def _():
        for c in range(nc):
            cs = slice(c * LANES, (c + 1) * LANES)
            for r in range(SUBLANES):
                w = jnp.zeros((PEER_KEYS, LANES), F32)
                for h in range(PEER_HEADS):
                    sel = jnp.where(s2_ref[c, h] >= c1_ref[c, h, r:r + 1, :], b2_ref[c, h], 0.0)
                    w = w + sel * a1_ref[c, h, r:r + 1, :]
                rows = slice(r * PEER_KEYS, (r + 1) * PEER_KEYS)
                act[rows, cs] = (_gelu_tanh(st[rows, cs]) * w).astype(BF16)

    @pl.when(j >= -2)
    def _():
        acc[...] += jnp.dot(vt_ref[0], act[...], preferred_element_type=F32)

    @pl.when(j == pl.num_programs(1) - 1)
    def _():
        o_ref[...] = _rms(x1_ref[...] + g2_ref[0] * acc[...].T) * gf_ref[...]


def _peer(h2t, u_b, v_tiles, c1, a1, s2, b2, x1, ga2, g_final, seq, tm=512):
    d, t = h2t.shape
    te = PEER_TILE
    ne = u_b.shape[0] // te
    nh = c1.shape[1]
    nc = tm // LANES
    nb = seq // tm
    rows = lambda: pl.BlockSpec((nc, nh, SUBLANES, LANES), lambda i, j: (i, 0, j, 0))
    full = lambda: pl.BlockSpec((nc, nh, PEER_KEYS, LANES), lambda i, j: (i, 0, 0, 0),
                                pipeline_mode=pl.Buffered(1))
    return pl.pallas_call(
        functools.partial(_peer_kernel, nc=nc),
        grid=(t // tm, ne),
        in_specs=[pl.BlockSpec((d, tm), lambda i, j: (0, i)),
                  pl.BlockSpec((te, d), lambda i, j: (j, 0)),
                  pl.BlockSpec((1, d, te), lambda i, j: (j, 0, 0)),
                  rows(), rows(), full(), full(),
                  pl.BlockSpec((tm, d), lambda i, j: (i, 0)),
                  pl.BlockSpec((1, 1, d), lambda i, j: (i // nb, 0, 0)),
                  pl.BlockSpec((1, d), lambda i, j: (0, 0))],
        out_specs=pl.BlockSpec((tm, d), lambda i, j: (i, 0)),
        out_shape=jax.ShapeDtypeStruct((t, d), F32),
        scratch_shapes=[pltpu.VMEM((d, tm), F32), pltpu.VMEM((te, tm), F32),
                        pltpu.VMEM((te, tm), BF16)],
        compiler_params=_params(("arbitrary", "arbitrary")),
        name="peer",
    )(h2t, u_b, v_tiles, c1, a1, s2, b2, x1, ga2, g_final.reshape(1, d))


def kernel(x, c, w_ada, b_ada, g_mix, w_in, b_f, conv_w, conv_b, lru_w_a, lru_b_a, lru_w_x,
           lru_b_x, lru_lambda, g_attn_out, g_lru_out, w_out, g_ffn, peer_w_q, peer_k1,
           peer_k2, peer_u, peer_v, g_final):
    bsz, seq, d = x.shape
    depth = w_ada.shape[0]
    assert depth == 1, "the final residual + norm is fused into the PEER kernel of a single layer"
    x2 = x.reshape(bsz * seq, d)
    off_f = 3 * ATTN_WIDTH
    off_l = off_f + ATTN_HEADS

    for l in range(depth):
        mod = _ada(c, w_ada[l], b_ada[l])
        sh1, sc1, ga1, sh2, sc2, ga2 = [m.reshape(bsz, 1, d) for m in jnp.split(mod, N_ADA, axis=-1)]

        h = _modnorm(x2, g_mix[l], sc1, sh1, seq)
        w_l = w_in[l]
        q_scale = jnp.where(jnp.arange(2 * ATTN_WIDTH) < ATTN_WIDTH, LOG2E * HEAD_DIM ** -0.5, 1.0)
        qk = _matmul_colscale(h, w_l[:, :2 * ATTN_WIDTH].astype(BF16), q_scale.astype(F32), BF16,
                              name="proj_qk")
        vt = _matmul_t(h, w_l[:, 2 * ATTN_WIDTH:off_f].T.astype(BF16), BF16, tm=ATTN_TQ,
                       name="proj_vt")
        w_f = jnp.pad(w_l[:, off_f:off_l], ((0, 0), (0, LANES - ATTN_HEADS))).astype(BF16)
        fl = _matmul(h, w_f, F32, name="proj_f")
        lxy = _matmul(h, w_l[:, off_l:].astype(BF16), F32, name="proj_lru")

        bf = jnp.pad(b_f[l], (0, LANES - ATTN_HEADS)).reshape(1, LANES)
        cum_rep, cum_row = _cum(fl, bf, bsz, seq)
        attn = _attn(qk, vt, cum_rep, cum_row, bsz, seq)
        lru_n = _lru(lxy, conv_w[l], conv_b[l], lru_w_a[l], lru_b_a[l], lru_w_x[l], lru_b_x[l],
                     lru_lambda[l], g_lru_out[l], bsz, seq)

        w_o = w_out[l].astype(BF16)
        x1, h2t, qp3 = _outproj(attn, lru_n, x2, g_attn_out[l], w_o[:ATTN_WIDTH],
                                w_o[ATTN_WIDTH:], ga1, sc2, sh2, g_ffn[l],
                                peer_w_q[l].astype(BF16), PEER_HEADS, seq)
        c1, a1, s2, b2 = _topk(qp3, peer_k1[l].astype(BF16), peer_k2[l].astype(BF16))
        v_tiles = peer_v[l].reshape(-1, PEER_TILE, d).transpose(0, 2, 1).astype(BF16)
        out = _peer(h2t, peer_u[l].astype(BF16), v_tiles, c1, a1, s2, b2, x1, ga2, g_final, seq)
    return out.reshape(bsz, seq, d)
```

```python
import functools

import jax
import jax.numpy as jnp
from jax import lax
from jax.experimental import pallas as pl
from jax.experimental.pallas import tpu as pltpu

F32 = jnp.float32
BF16 = jnp.bfloat16

ATTN_HEADS = 8
HEAD_DIM = 128
ATTN_WIDTH = ATTN_HEADS * HEAD_DIM
LRU_BLOCKS = 8
LRU_BLOCK_DIM = 128
LRU_WIDTH = LRU_BLOCKS * LRU_BLOCK_DIM
CONV_WIDTH = 4
LRU_C = 8.0
PEER_HEADS = 8
PEER_KEYS = 128
PEER_HALF = 128
PEER_TOPK = 16
N_ADA = 6
EPS = 1e-6
LANES = 128
SUBLANES = 8
NEG_BIG = -1e30
LOG2E = 1.4426950408889634
VMEM_LIMIT = 56 * 1024 * 1024

_NT = (((1,), (1,)), ((), ()))


def _params(sem, vmem=VMEM_LIMIT):
    return pltpu.CompilerParams(dimension_semantics=sem, vmem_limit_bytes=vmem)


def _gelu_tanh(x):
    k1 = -2.0 * 0.7978845608028654 * LOG2E
    k3 = k1 * 0.044715
    return x / (1.0 + jnp.exp2(x * (k1 + k3 * (x * x))))


def _softplus(z):
    return jnp.maximum(z, 0.0) + jnp.log1p(jnp.exp(-jnp.abs(z)))


def _rms(x):
    return x * lax.rsqrt(jnp.mean(x * x, axis=-1, keepdims=True) + EPS)


def _ada_kernel(c_ref, w_ref, b_ref, o_ref):
    @pl.when(pl.program_id(0) == 0)
    def _():
        o_ref[...] = jnp.broadcast_to(b_ref[...], o_ref.shape)

    c = c_ref[...]
    sc = c * jax.nn.sigmoid(c)
    o_ref[...] += jnp.dot(sc.astype(BF16), w_ref[...].astype(BF16), preferred_element_type=F32)


def _ada(c, w, b, tk=256):
    bsz, d = c.shape
    n = w.shape[1]
    return pl.pallas_call(
        _ada_kernel,
        grid=(d // tk,),
        in_specs=[pl.BlockSpec((bsz, tk), lambda k: (0, k)),
                  pl.BlockSpec((tk, n), lambda k: (k, 0)),
                  pl.BlockSpec((1, n), lambda k: (0, 0))],
        out_specs=pl.BlockSpec((bsz, n), lambda k: (0, 0)),
        out_shape=jax.ShapeDtypeStruct((bsz, n), F32),
        compiler_params=_params(("arbitrary",)),
        name="ada",
    )(c, w, b.reshape(1, n))


def _modnorm_kernel(x_ref, g_ref, sc_ref, sh_ref, o_ref):
    y = _rms(x_ref[...]) * g_ref[...]
    o_ref[...] = (y * (1.0 + sc_ref[0]) + sh_ref[0]).astype(o_ref.dtype)


def _modnorm(x2, g, sc, sh, seq, tm=512):
    t, d = x2.shape
    nb = seq // tm
    return pl.pallas_call(
        _modnorm_kernel,
        grid=(t // tm,),
        in_specs=[pl.BlockSpec((tm, d), lambda i: (i, 0)),
                  pl.BlockSpec((1, d), lambda i: (0, 0)),
                  pl.BlockSpec((1, 1, d), lambda i: (i // nb, 0, 0)),
                  pl.BlockSpec((1, 1, d), lambda i: (i // nb, 0, 0))],
        out_specs=pl.BlockSpec((tm, d), lambda i: (i, 0)),
        out_shape=jax.ShapeDtypeStruct((t, d), BF16),
        compiler_params=_params(("arbitrary",)),
        name="modnorm",
    )(x2, g.reshape(1, d), sc, sh)


def _matmul_kernel(a_ref, w_ref, o_ref):
    o_ref[...] = jnp.dot(a_ref[...], w_ref[...],
                         preferred_element_type=F32).astype(o_ref.dtype)


def _matmul(a, w, out_dtype, tm=512, tn=1024, name="matmul"):
    t, k = a.shape
    n = w.shape[1]
    tn = min(tn, n)
    return pl.pallas_call(
        _matmul_kernel,
        grid=(n // tn, t // tm),
        in_specs=[pl.BlockSpec((tm, k), lambda j, i: (i, 0)),
                  pl.BlockSpec((k, tn), lambda j, i: (0, j))],
        out_specs=pl.BlockSpec((tm, tn), lambda j, i: (i, j)),
        out_shape=jax.ShapeDtypeStruct((t, n), out_dtype),
        compiler_params=_params(("arbitrary", "arbitrary")),
        name=name,
    )(a, w)


def _matmul_colscale_kernel(a_ref, w_ref, cs_ref, o_ref):
    acc = jnp.dot(a_ref[...], w_ref[...], preferred_element_type=F32)
    o_ref[...] = (acc * cs_ref[...]).astype(o_ref.dtype)


def _matmul_colscale(a, w, col_scale, out_dtype, tm=512, tn=1024, name="matmul_colscale"):
    t, k = a.shape
    n = w.shape[1]
    return pl.pallas_call(
        _matmul_colscale_kernel,
        grid=(n // tn, t // tm),
        in_specs=[pl.BlockSpec((tm, k), lambda j, i: (i, 0)),
                  pl.BlockSpec((k, tn), lambda j, i: (0, j)),
                  pl.BlockSpec((1, tn), lambda j, i: (0, j))],
        out_specs=pl.BlockSpec((tm, tn), lambda j, i: (i, j)),
        out_shape=jax.ShapeDtypeStruct((t, n), out_dtype),
        compiler_params=_params(("arbitrary", "arbitrary")),
        name=name,
    )(a, w, col_scale.reshape(1, n))


def _matmul_t_kernel(a_ref, wt_ref, o_ref):
    o_ref[0] = lax.dot_general(wt_ref[...], a_ref[...], _NT,
                               preferred_element_type=F32).astype(o_ref.dtype)


def _matmul_t(a, wt, out_dtype, tm=512, tn=1024, name="matmul_t"):
    t, k = a.shape
    n = wt.shape[0]
    tn = min(tn, n)
    return pl.pallas_call(
        _matmul_t_kernel,
        grid=(n // tn, t // tm),
        in_specs=[pl.BlockSpec((tm, k), lambda j, i: (i, 0)),
                  pl.BlockSpec((tn, k), lambda j, i: (j, 0))],
        out_specs=pl.BlockSpec((1, tn, tm), lambda j, i: (i, j, 0)),
        out_shape=jax.ShapeDtypeStruct((t // tm, n, tm), out_dtype),
        compiler_params=_params(("arbitrary", "arbitrary")),
        name=name,
    )(a, wt)


def _cum_kernel(fl_ref, bf_ref, rep_ref, row_ref, carry_ref, *, ts):
    @pl.when(pl.program_id(1) == 0)
    def _():
        carry_ref[...] = jnp.zeros_like(carry_ref)

    lf = -_softplus(-(fl_ref[...] + bf_ref[...]))
    rows = lax.broadcasted_iota(jnp.int32, lf.shape, 0)
    s = 1
    while s < ts:
        lf = lf + jnp.where(rows >= s, pltpu.roll(lf, s, axis=0), 0.0)
        s *= 2
    cum = lf + carry_ref[...]
    carry_ref[...] = cum[ts - 1:ts, :]
    cum2 = cum * LOG2E
    for h in range(ATTN_HEADS):
        rep_ref[0, h] = jnp.broadcast_to(cum2[:, h:h + 1], (ts, LANES))
    row_ref[0] = cum2.T[0:ATTN_HEADS, :]


def _cum(fl, bf, bsz, seq, ts=512):
    ns = seq // ts
    return pl.pallas_call(
        functools.partial(_cum_kernel, ts=ts),
        grid=(bsz, ns),
        in_specs=[pl.BlockSpec((ts, LANES), lambda b, s: (b * ns + s, 0)),
                  pl.BlockSpec((1, LANES), lambda b, s: (0, 0))],
        out_specs=[pl.BlockSpec((1, ATTN_HEADS, ts, LANES), lambda b, s: (b, 0, s, 0)),
                   pl.BlockSpec((1, ATTN_HEADS, ts), lambda b, s: (b, 0, s))],
        out_shape=[jax.ShapeDtypeStruct((bsz, ATTN_HEADS, seq, LANES), F32),
                   jax.ShapeDtypeStruct((bsz, ATTN_HEADS, seq), F32)],
        scratch_shapes=[pltpu.VMEM((1, LANES), F32)],
        compiler_params=_params(("arbitrary", "arbitrary")),
        name="cum",
    )(fl, bf)


def _attn_kernel(q_ref, k_ref, vt_ref, cq_ref, ck_ref, o_ref, m_ref, l_ref, acc_ref, *, tq):
    qi = pl.program_id(2)
    q = q_ref[...]
    cq = cq_ref[0, 0, 0]
    m_ref[...] = jnp.full_like(m_ref, NEG_BIG)
    l_ref[...] = jnp.zeros_like(l_ref)
    acc_ref[...] = jnp.zeros_like(acc_ref)
    reps = tq // LANES

    def step(j, nblk, causal):
        tk = nblk * tq
        ks = pl.multiple_of(j * tq, tq)
        k = k_ref[pl.ds(ks, tk), :]
        ck = ck_ref[0, 0, pl.ds(ks, tk), :]
        s = lax.dot_general(k, q, _NT, preferred_element_type=F32)
        s = s + (cq - jnp.concatenate([ck] * reps, axis=1))
        if causal:
            key = lax.broadcasted_iota(jnp.int32, (tk, tq), 0) - (nblk - 1) * tq
            qry = lax.broadcasted_iota(jnp.int32, (tk, tq), 1)
            s = jnp.where(key <= qry, s, NEG_BIG)
        m_prev = m_ref[...]
        m_new = jnp.maximum(m_prev, jnp.max(s, axis=0, keepdims=True))
        alpha = jnp.exp2(m_prev - m_new)
        p = jnp.exp2(s - m_new)
        l_ref[...] = alpha * l_ref[...] + jnp.sum(p, axis=0, keepdims=True)
        pb = p.astype(BF16)
        pv = jnp.dot(vt_ref[j], pb[0:tq], preferred_element_type=F32)
        for b in range(1, nblk):
            pv = pv + jnp.dot(vt_ref[j + b], pb[b * tq:(b + 1) * tq], preferred_element_type=F32)
        acc_ref[...] = alpha * acc_ref[...] + pv
        m_ref[...] = m_new

    def quad(i, carry):
        step(4 * i, 4, False)
        return carry

    lax.fori_loop(0, qi // 4, quad, 0)
    rest = 4 * (qi // 4)

    @pl.when(qi % 4 >= 2)
    def _():
        step(rest, 2, False)

    @pl.when(qi % 2 == 1)
    def _():
        step(qi - 1, 2, True)

    @pl.when(qi % 2 == 0)
    def _():
        step(qi, 1, True)

    o_ref[...] = (acc_ref[...] / l_ref[...]).T


ATTN_TQ = 512


def _attn(qk, vt, cum_rep, cum_row, bsz, seq, tq=ATTN_TQ):
    t = qk.shape[0]
    nq = seq // tq
    h = ATTN_HEADS
    cq = cum_row.reshape(bsz, h, nq, 1, tq)
    return pl.pallas_call(
        functools.partial(_attn_kernel, tq=tq),
        grid=(bsz, h, nq),
        in_specs=[pl.BlockSpec((tq, HEAD_DIM), lambda b, hh, i: (b * nq + i, hh)),
                  pl.BlockSpec((seq, HEAD_DIM), lambda b, hh, i: (b, h + hh)),
                  pl.BlockSpec((nq, HEAD_DIM, tq), lambda b, hh, i: (b, hh, 0)),
                  pl.BlockSpec((1, 1, 1, 1, tq), lambda b, hh, i: (b, hh, i, 0, 0)),
                  pl.BlockSpec((1, 1, seq, LANES), lambda b, hh, i: (b, hh, 0, 0))],
        out_specs=pl.BlockSpec((tq, HEAD_DIM), lambda b, hh, i: (b * nq + i, hh)),
        out_shape=jax.ShapeDtypeStruct((t, ATTN_WIDTH), F32),
        scratch_shapes=[pltpu.VMEM((1, tq), F32), pltpu.VMEM((1, tq), F32),
                        pltpu.VMEM((HEAD_DIM, tq), F32)],
        compiler_params=_params(("arbitrary", "arbitrary", "arbitrary")),
        name="attn",
    )(qk, qk, vt, cq, cum_rep)


def _lru_kernel(lx_ref, ly_ref, cw_ref, cb_ref, wa_ref, ba_ref, wx_ref, bx_ref, lam_ref,
                g_ref, o_ref, xbuf, hc, *, ts):
    @pl.when(pl.program_id(1) == 0)
    def _():
        xbuf[0:SUBLANES, :] = jnp.zeros((SUBLANES, LRU_WIDTH), F32)
        hc[...] = jnp.zeros_like(hc)

    xbuf[SUBLANES:, :] = lx_ref[...]
    xc = cb_ref[...]
    for k in range(CONV_WIDTH):
        off = SUBLANES - (CONV_WIDTH - 1) + k
        xc = xc + cw_ref[k:k + 1, :] * xbuf[off:off + ts, :]
    xbuf[0:SUBLANES, :] = xbuf[ts:ts + SUBLANES, :]

    xcb = xc.astype(BF16)
    ra, rx = [], []
    for n in range(LRU_BLOCKS):
        xs = xcb[:, n * LRU_BLOCK_DIM:(n + 1) * LRU_BLOCK_DIM]
        ra.append(jnp.dot(xs, wa_ref[n], preferred_element_type=F32))
        rx.append(jnp.dot(xs, wx_ref[n], preferred_element_type=F32))
    r = jax.nn.sigmoid(jnp.concatenate(ra, axis=1) + ba_ref[...])
    ig = jax.nn.sigmoid(jnp.concatenate(rx, axis=1) + bx_ref[...])
    log_a = (-LRU_C) * r * _softplus(-lam_ref[...])
    a = jnp.exp(log_a)
    b = jnp.sqrt(-jnp.tanh(log_a) * (a * a + 1.0)) * (ig * xc)

    rows = lax.broadcasted_iota(jnp.int32, a.shape, 0)
    s = 1
    while s < ts:
        keep = rows >= s
        b = jnp.where(keep, a * pltpu.roll(b, s, axis=0) + b, b)
        a = jnp.where(keep, a * pltpu.roll(a, s, axis=0), a)
        s *= 2
    hseq = b + a * hc[...]
    hc[...] = hseq[ts - 1:ts, :]

    lru = hseq * _gelu_tanh(ly_ref[...])
    o_ref[...] = (_rms(lru) * g_ref[...]).astype(o_ref.dtype)


def _lru(lxy, conv_w, conv_b, w_a, b_a, w_x, b_x, lam, g, bsz, seq, ts=256):
    t = lxy.shape[0]
    ns = seq // ts
    w = LRU_WIDTH
    vec = lambda: pl.BlockSpec((1, w), lambda b, s: (0, 0))
    blk = lambda: pl.BlockSpec((LRU_BLOCKS, LRU_BLOCK_DIM, LRU_BLOCK_DIM), lambda b, s: (0, 0, 0))
    return pl.pallas_call(
        functools.partial(_lru_kernel, ts=ts),
        grid=(bsz, ns),
        in_specs=[pl.BlockSpec((ts, w), lambda b, s: (b * ns + s, 0)),
                  pl.BlockSpec((ts, w), lambda b, s: (b * ns + s, 1)),
                  pl.BlockSpec((CONV_WIDTH, w), lambda b, s: (0, 0)),
                  vec(), blk(), vec(), blk(), vec(), vec(), vec()],
        out_specs=pl.BlockSpec((ts, w), lambda b, s: (b * ns + s, 0)),
        out_shape=jax.ShapeDtypeStruct((t, w), BF16),
        scratch_shapes=[pltpu.VMEM((ts + SUBLANES, w), F32), pltpu.VMEM((1, w), F32)],
        compiler_params=_params(("arbitrary", "arbitrary")),
        name="lru",
    )(lxy, lxy, conv_w, conv_b.reshape(1, w), w_a.astype(BF16), b_a.reshape(1, w),
      w_x.astype(BF16), b_x.reshape(1, w), lam.reshape(1, w), g.reshape(1, w))


def _outproj_kernel(at_ref, lr_ref, x_ref, ga_ref, woa_ref, wol_ref, g1_ref, sc_ref, sh_ref,
                    gf_ref, wq_ref, x1_ref, h2t_ref, qp_ref):
    an = (_rms(at_ref[...]) * ga_ref[...]).astype(BF16)
    mix = jnp.dot(an, woa_ref[...], preferred_element_type=F32)
    mix = mix + jnp.dot(lr_ref[...], wol_ref[...], preferred_element_type=F32)
    x1 = x_ref[...] + g1_ref[0] * mix
    x1_ref[...] = x1
    h2 = (_rms(x1) * gf_ref[...]) * (1.0 + sc_ref[0]) + sh_ref[0]
    h2t_ref[...] = h2.T.astype(h2t_ref.dtype)
    qp = jnp.dot(h2.astype(BF16), wq_ref[...], preferred_element_type=F32).astype(qp_ref.dtype)
    width = qp_ref.shape[2]
    for h in range(qp_ref.shape[0]):
        qp_ref[h] = qp[:, h * width:(h + 1) * width]


def _outproj(attn, lru_n, x2, g_attn, w_oa, w_ol, ga1, sc2, sh2, g_ffn, w_q, nh, seq, tm=256):
    t, d = x2.shape
    nb = seq // tm
    aw = attn.shape[1]
    lw = lru_n.shape[1]
    nq = w_q.shape[1]
    mod = lambda: pl.BlockSpec((1, 1, d), lambda i: (i // nb, 0, 0))
    return pl.pallas_call(
        _outproj_kernel,
        grid=(t // tm,),
        in_specs=[pl.BlockSpec((tm, aw), lambda i: (i, 0)),
                  pl.BlockSpec((tm, lw), lambda i: (i, 0)),
                  pl.BlockSpec((tm, d), lambda i: (i, 0)),
                  pl.BlockSpec((1, aw), lambda i: (0, 0)),
                  pl.BlockSpec((aw, d), lambda i: (0, 0)),
                  pl.BlockSpec((lw, d), lambda i: (0, 0)),
                  mod(), mod(), mod(),
                  pl.BlockSpec((1, d), lambda i: (0, 0)),
                  pl.BlockSpec((d, nq), lambda i: (0, 0))],
        out_specs=[pl.BlockSpec((tm, d), lambda i: (i, 0)),
                   pl.BlockSpec((d, tm), lambda i: (0, i)),
                   pl.BlockSpec((nh, tm, nq // nh), lambda i: (0, i, 0))],
        out_shape=[jax.ShapeDtypeStruct((t, d), F32), jax.ShapeDtypeStruct((d, t), BF16),
                   jax.ShapeDtypeStruct((nh, t, nq // nh), BF16)],
        compiler_params=_params(("arbitrary",)),
        name="outproj",
    )(attn, lru_n, x2, g_attn.reshape(1, aw), w_oa, w_ol, ga1, sc2, sh2, g_ffn.reshape(1, d), w_q)


def _sort16_pairs():
    pairs = []

    def merge(lo, hi, r):
        step = r * 2
        if step < hi - lo:
            merge(lo, hi, step)
            merge(lo + r, hi, step)
            for i in range(lo + r, hi - r, step):
                pairs.append((i, i + r))
        else:
            pairs.append((lo, lo + r))

    def sort(lo, hi):
        if hi - lo >= 1:
            mid = lo + (hi - lo) // 2
            sort(lo, mid)
            sort(mid + 1, hi)
            merge(lo, hi, 1)

    sort(0, PEER_TOPK - 1)
    return pairs


_SORT16 = _sort16_pairs()


def _cex(v, i, j):
    hi, lo = jnp.maximum(v[i], v[j]), jnp.minimum(v[i], v[j])
    v[i], v[j] = hi, lo


def _sort16(v):
    v = list(v)
    for i, j in _SORT16:
        _cex(v, i, j)
    return v


def _bitonic_merge16(v):
    v = list(v)
    d = PEER_TOPK // 2
    while d >= 1:
        for i in range(PEER_TOPK):
            if i & d == 0:
                _cex(v, i, i + d)
        d //= 2
    return v


def _top16_of_union(a, b):
    return [jnp.maximum(a[i], b[PEER_TOPK - 1 - i]) for i in range(PEER_TOPK)]


def _top16_sorted(s):
    v = _sort16([s[i * SUBLANES:(i + 1) * SUBLANES, :] for i in range(PEER_TOPK)])
    for shift in (4, 2, 1):
        other = [pltpu.roll(x, shift, axis=0) for x in v]
        v = _bitonic_merge16(_top16_of_union(v, other))
    return v


def _threshold_stats(v1, v2):
    k = PEER_TOPK
    cells = {(i, j): v1[i] + v2[j] for i in range(k) for j in range(k) if (i + 1) * (j + 1) <= k}
    row0 = [cells[0, j] for j in range(k)]
    col0 = [cells[i, 0] for i in range(1, k)]
    rest = ([(1, j) for j in range(1, 8)] + [(i, 1) for i in range(2, 8)]
            + [(2, j) for j in range(2, 5)] + [(i, 2) for i in range(3, 5)] + [(3, 3)])
    rest = [cells[ij] for ij in rest]
    ab = [row0[0]] + [jnp.maximum(row0[i], col0[k - 1 - i]) for i in range(1, k)]
    ab = _bitonic_merge16(ab)
    c1 = _sort16(rest[:k])
    t = _bitonic_merge16(_top16_of_union(ab, c1))
    y = list(rest[k:])
    _cex(y, 0, 1), _cex(y, 1, 2), _cex(y, 0, 1)
    thr = jnp.minimum(jnp.minimum(t[12], jnp.maximum(t[13], y[2])),
                      jnp.minimum(jnp.maximum(t[14], y[1]), jnp.maximum(t[15], y[0])))
    top = row0[0]
    z = jnp.zeros_like(top)
    for c in cells.values():
        z = z + jnp.where(c >= thr, jnp.exp(c - top), 0.0)
    t = []
    for j in range(k):
        tj = jnp.full_like(top, jnp.inf)
        for i in range(k // (j + 1)):
            tj = jnp.where(cells[i, j] >= thr, v1[i], tj)
        t.append(tj)
    return z, t


def _topk_kernel(qp_ref, k1_ref, k2_ref, c1_ref, a1_ref, s2_ref, b2_ref):
    reps = PEER_KEYS // SUBLANES

    def head(h):
        q = qp_ref[h]
        s1 = lax.dot_general(k1_ref[h], q[:, :PEER_HALF], _NT, preferred_element_type=F32)
        s2 = lax.dot_general(k2_ref[h], q[:, PEER_HALF:], _NT, preferred_element_type=F32)
        v1 = _top16_sorted(s1)
        v2 = _top16_sorted(s2)
        z, t = _threshold_stats(v1, v2)
        c1 = jnp.full_like(s1, jnp.inf)
        for tj, v in zip(t, v2):
            c1 = jnp.where(s1 >= jnp.tile(tj, (reps, 1)), jnp.tile(v, (reps, 1)), c1)
        c1_ref[0, h] = c1
        s2_ref[0, h] = s2
        a1_ref[0, h] = jnp.exp(s1 - v1[0][0:1, :]) / z[0:1, :]
        b2_ref[0, h] = jnp.exp(s2 - v2[0][0:1, :])

    def two_heads(i, carry):
        head(2 * i)
        head(2 * i + 1)
        return carry

    lax.fori_loop(0, PEER_HEADS // 2, two_heads, 0)


def _topk(qp3, k1, k2):
    nh, t, qd = qp3.shape
    nc = t // LANES
    big = lambda: pl.BlockSpec((1, nh, PEER_KEYS, LANES), lambda i: (i, 0, 0, 0))
    keys = lambda: pl.BlockSpec((nh, PEER_KEYS, PEER_HALF), lambda i: (0, 0, 0))
    sds = jax.ShapeDtypeStruct((nc, nh, PEER_KEYS, LANES), F32)
    return pl.pallas_call(
        _topk_kernel,
        grid=(nc,),
        in_specs=[pl.BlockSpec((nh, LANES, qd), lambda i: (0, i, 0)), keys(), keys()],
        out_specs=[big(), big(), big(), big()],
        out_shape=[sds, sds, sds, sds],
        compiler_params=_params(("arbitrary",)),
        name="topk",
    )(qp3, k1, k2)


PEER_TILE = SUBLANES * PEER_KEYS


def _peer_kernel(ht_ref, u_ref, vt_ref, c1_ref, a1_ref, s2_ref, b2_ref, x1_ref, g2_ref, gf_ref,
                 o_ref, acc, st, act, *, nc):
    j = pl.program_id(1)

    @pl.when(j == 0)
    def _():
        acc[...] = jnp.zeros_like(acc)

    @pl.when(j >= 0)
    def _():
        st[...] = jnp.dot(u_ref[...], ht_ref[...], preferred_element_type=F32)

    @pl.when(j >= -1)
    def _():
        for c in range(nc):
            cs = slice(c * LANES, (c + 1) * LANES)
            for r in range(SUBLANES):
                w = None
                for h in range(PEER_HEADS):
                    sel = jnp.where(s2_ref[c, h] >= c1_ref[c, h, r:r + 1, :], b2_ref[c, h], 0.0)
                    term = sel * a1_ref[c, h, r:r + 1, :]
                    w = term if w is None else w + term
                rows = slice(r * PEER_KEYS, (r + 1) * PEER_KEYS)
                act[rows, cs] = (_gelu_tanh(st[rows, cs]) * w).astype(BF16)

    @pl.when(j >= -2)
    def _():
        acc[...] += jnp.dot(vt_ref[0], act[...], preferred_element_type=F32)

    @pl.when(j == pl.num_programs(1) - 1)
    def _():
        o_ref[...] = _rms(x1_ref[...] + g2_ref[0] * acc[...].T) * gf_ref[...]


def _peer(h2t, u_b, v_tiles, c1, a1, s2, b2, x1, ga2, g_final, seq, tm=512):
    d, t = h2t.shape
    te = PEER_TILE
    ne = u_b.shape[0] // te
    nh = c1.shape[1]
    nc = tm // LANES
    nb = seq // tm
    rows = lambda: pl.BlockSpec((nc, nh, SUBLANES, LANES), lambda i, j: (i, 0, j, 0))
    full = lambda: pl.BlockSpec((nc, nh, PEER_KEYS, LANES), lambda i, j: (i, 0, 0, 0),
                                pipeline_mode=pl.Buffered(1))
    return pl.pallas_call(
        functools.partial(_peer_kernel, nc=nc),
        grid=(t // tm, ne),
        in_specs=[pl.BlockSpec((d, tm), lambda i, j: (0, i)),
                  pl.BlockSpec((te, d), lambda i, j: (j, 0)),
                  pl.BlockSpec((1, d, te), lambda i, j: (j, 0, 0)),
                  rows(), rows(), full(), full(),
                  pl.BlockSpec((tm, d), lambda i, j: (i, 0)),
                  pl.BlockSpec((1, 1, d), lambda i, j: (i // nb, 0, 0)),
                  pl.BlockSpec((1, d), lambda i, j: (0, 0))],
        out_specs=pl.BlockSpec((tm, d), lambda i, j: (i, 0)),
        out_shape=jax.ShapeDtypeStruct((t, d), F32),
        scratch_shapes=[pltpu.VMEM((d, tm), F32), pltpu.VMEM((te, tm), F32),
                        pltpu.VMEM((te, tm), BF16)],
        compiler_params=_params(("arbitrary", "arbitrary")),
        name="peer",
    )(h2t, u_b, v_tiles, c1, a1, s2, b2, x1, ga2, g_final.reshape(1, d))


def kernel(x, c, w_ada, b_ada, g_mix, w_in, b_f, conv_w, conv_b, lru_w_a, lru_b_a, lru_w_x,
           lru_b_x, lru_lambda, g_attn_out, g_lru_out, w_out, g_ffn, peer_w_q, peer_k1,
           peer_k2, peer_u, peer_v, g_final):
    bsz, seq, d = x.shape
    depth = w_ada.shape[0]
    assert depth == 1, "the final residual + norm is fused into the PEER kernel of a single layer"
    x2 = x.reshape(bsz * seq, d)
    off_f = 3 * ATTN_WIDTH
    off_l = off_f + ATTN_HEADS

    for l in range(depth):
        mod = _ada(c, w_ada[l], b_ada[l])
        sh1, sc1, ga1, sh2, sc2, ga2 = [m.reshape(bsz, 1, d) for m in jnp.split(mod, N_ADA, axis=-1)]

        h = _modnorm(x2, g_mix[l], sc1, sh1, seq)
        w_l = w_in[l]
        q_scale = jnp.where(jnp.arange(2 * ATTN_WIDTH) < ATTN_WIDTH, LOG2E * HEAD_DIM ** -0.5, 1.0)
        qk = _matmul_colscale(h, w_l[:, :2 * ATTN_WIDTH].astype(BF16), q_scale.astype(F32), BF16,
                              name="proj_qk")
        vt = _matmul_t(h, w_l[:, 2 * ATTN_WIDTH:off_f].T.astype(BF16), BF16, tm=ATTN_TQ,
                       name="proj_vt")
        w_f = jnp.pad(w_l[:, off_f:off_l], ((0, 0), (0, LANES - ATTN_HEADS))).astype(BF16)
        fl = _matmul(h, w_f, F32, name="proj_f")
        lxy = _matmul(h, w_l[:, off_l:].astype(BF16), F32, name="proj_lru")

        bf = jnp.pad(b_f[l], (0, LANES - ATTN_HEADS)).reshape(1, LANES)
        cum_rep, cum_row = _cum(fl, bf, bsz, seq)
        attn = _attn(qk, vt, cum_rep, cum_row, bsz, seq)
        lru_n = _lru(lxy, conv_w[l], conv_b[l], lru_w_a[l], lru_b_a[l], lru_w_x[l], lru_b_x[l],
                     lru_lambda[l], g_lru_out[l], bsz, seq)

        w_o = w_out[l].astype(BF16)
        x1, h2t, qp3 = _outproj(attn, lru_n, x2, g_attn_out[l], w_o[:ATTN_WIDTH],
                                w_o[ATTN_WIDTH:], ga1, sc2, sh2, g_ffn[l],
                                peer_w_q[l].astype(BF16), PEER_HEADS, seq)
        c1, a1, s2, b2 = _topk(qp3, peer_k1[l].astype(BF16), peer_k2[l].astype(BF16))
        v_tiles = peer_v[l].reshape(-1, PEER_TILE, d).transpose(0, 2, 1).astype(BF16)
        out = _peer(h2t, peer_u[l].astype(BF16), v_tiles, c1, a1, s2, b2, x1, ga2, g_final, seq)
    return out.reshape(bsz, seq, d)
```

```python
import functools

import jax
import jax.numpy as jnp
from jax import lax
from jax.experimental import pallas as pl
from jax.experimental.pallas import tpu as pltpu

F32 = jnp.float32
BF16 = jnp.bfloat16

ATTN_HEADS = 8
HEAD_DIM = 128
ATTN_WIDTH = ATTN_HEADS * HEAD_DIM
LRU_BLOCKS = 8
LRU_BLOCK_DIM = 128
LRU_WIDTH = LRU_BLOCKS * LRU_BLOCK_DIM
CONV_WIDTH = 4
LRU_C = 8.0
PEER_HEADS = 8
PEER_KEYS = 128
PEER_HALF = 128
PEER_TOPK = 16
N_ADA = 6
EPS = 1e-6
LANES = 128
SUBLANES = 8
NEG_BIG = -1e30
LOG2E = 1.4426950408889634
VMEM_LIMIT = 56 * 1024 * 1024

_NT = (((1,), (1,)), ((), ()))


def _params(sem, vmem=VMEM_LIMIT):
    return pltpu.CompilerParams(dimension_semantics=sem, vmem_limit_bytes=vmem)


def _gelu_tanh(x):
    k1 = -2.0 * 0.7978845608028654 * LOG2E
    k3 = k1 * 0.044715
    return x / (1.0 + jnp.exp2(x * (k1 + k3 * (x * x))))


def _softplus(z):
    return jnp.maximum(z, 0.0) + jnp.log1p(jnp.exp(-jnp.abs(z)))


def _rms(x):
    return x * lax.rsqrt(jnp.mean(x * x, axis=-1, keepdims=True) + EPS)


def _ada_kernel(c_ref, w_ref, b_ref, o_ref):
    @pl.when(pl.program_id(0) == 0)
    def _():
        o_ref[...] = jnp.broadcast_to(b_ref[...], o_ref.shape)

    c = c_ref[...]
    sc = c * jax.nn.sigmoid(c)
    o_ref[...] += jnp.dot(sc.astype(BF16), w_ref[...].astype(BF16), preferred_element_type=F32)


def _ada(c, w, b, tk=256):
    bsz, d = c.shape
    n = w.shape[1]
    return pl.pallas_call(
        _ada_kernel,
        grid=(d // tk,),
        in_specs=[pl.BlockSpec((bsz, tk), lambda k: (0, k)),
                  pl.BlockSpec((tk, n), lambda k: (k, 0)),
                  pl.BlockSpec((1, n), lambda k: (0, 0))],
        out_specs=pl.BlockSpec((bsz, n), lambda k: (0, 0)),
        out_shape=jax.ShapeDtypeStruct((bsz, n), F32),
        compiler_params=_params(("arbitrary",)),
        name="ada",
    )(c, w, b.reshape(1, n))


def _modnorm_kernel(x_ref, g_ref, sc_ref, sh_ref, o_ref):
    y = _rms(x_ref[...]) * g_ref[...]
    o_ref[...] = (y * (1.0 + sc_ref[0]) + sh_ref[0]).astype(o_ref.dtype)


def _modnorm(x2, g, sc, sh, seq, tm=512):
    t, d = x2.shape
    nb = seq // tm
    return pl.pallas_call(
        _modnorm_kernel,
        grid=(t // tm,),
        in_specs=[pl.BlockSpec((tm, d), lambda i: (i, 0)),
                  pl.BlockSpec((1, d), lambda i: (0, 0)),
                  pl.BlockSpec((1, 1, d), lambda i: (i // nb, 0, 0)),
                  pl.BlockSpec((1, 1, d), lambda i: (i // nb, 0, 0))],
        out_specs=pl.BlockSpec((tm, d), lambda i: (i, 0)),
        out_shape=jax.ShapeDtypeStruct((t, d), BF16),
        compiler_params=_params(("arbitrary",)),
        name="modnorm",
    )(x2, g.reshape(1, d), sc, sh)


def _matmul_kernel(a_ref, w_ref, o_ref):
    o_ref[...] = jnp.dot(a_ref[...], w_ref[...],
                         preferred_element_type=F32).astype(o_ref.dtype)


def _matmul(a, w, out_dtype, tm=512, tn=1024, name="matmul"):
    t, k = a.shape
    n = w.shape[1]
    tn = min(tn, n)
    return pl.pallas_call(
        _matmul_kernel,
        grid=(n // tn, t // tm),
        in_specs=[pl.BlockSpec((tm, k), lambda j, i: (i, 0)),
                  pl.BlockSpec((k, tn), lambda j, i: (0, j))],
        out_specs=pl.BlockSpec((tm, tn), lambda j, i: (i, j)),
        out_shape=jax.ShapeDtypeStruct((t, n), out_dtype),
        compiler_params=_params(("arbitrary", "arbitrary")),
        name=name,
    )(a, w)


def _matmul_colscale_kernel(a_ref, w_ref, cs_ref, o_ref):
    acc = jnp.dot(a_ref[...], w_ref[...], preferred_element_type=F32)
    o_ref[...] = (acc * cs_ref[...]).astype(o_ref.dtype)


def _matmul_colscale(a, w, col_scale, out_dtype, tm=512, tn=1024, name="matmul_colscale"):
    t, k = a.shape
    n = w.shape[1]
    return pl.pallas_call(
        _matmul_colscale_kernel,
        grid=(n // tn, t // tm),
        in_specs=[pl.BlockSpec((tm, k), lambda j, i: (i, 0)),
                  pl.BlockSpec((k, tn), lambda j, i: (0, j)),
                  pl.BlockSpec((1, tn), lambda j, i: (0, j))],
        out_specs=pl.BlockSpec((tm, tn), lambda j, i: (i, j)),
        out_shape=jax.ShapeDtypeStruct((t, n), out_dtype),
        compiler_params=_params(("arbitrary", "arbitrary")),
        name=name,
    )(a, w, col_scale.reshape(1, n))


def _matmul_t_kernel(a_ref, wt_ref, o_ref):
    o_ref[0] = lax.dot_general(wt_ref[...], a_ref[...], _NT,
                               preferred_element_type=F32).astype(o_ref.dtype)


def _matmul_t(a, wt, out_dtype, tm=512, tn=1024, name="matmul_t"):
    t, k = a.shape
    n = wt.shape[0]
    tn = min(tn, n)
    return pl.pallas_call(
        _matmul_t_kernel,
        grid=(n // tn, t // tm),
        in_specs=[pl.BlockSpec((tm, k), lambda j, i: (i, 0)),
                  pl.BlockSpec((tn, k), lambda j, i: (j, 0))],
        out_specs=pl.BlockSpec((1, tn, tm), lambda j, i: (i, j, 0)),
        out_shape=jax.ShapeDtypeStruct((t // tm, n, tm), out_dtype),
        compiler_params=_params(("arbitrary", "arbitrary")),
        name=name,
    )(a, wt)


def _cum_kernel(fl_ref, bf_ref, rep_ref, row_ref, carry_ref, *, ts):
    @pl.when(pl.program_id(1) == 0)
    def _():
        carry_ref[...] = jnp.zeros_like(carry_ref)

    lf = -_softplus(-(fl_ref[...] + bf_ref[...]))
    rows = lax.broadcasted_iota(jnp.int32, lf.shape, 0)
    s = 1
    while s < ts:
        lf = lf + jnp.where(rows >= s, pltpu.roll(lf, s, axis=0), 0.0)
        s *= 2
    cum = lf + carry_ref[...]
    carry_ref[...] = cum[ts - 1:ts, :]
    cum2 = cum * LOG2E
    for h in range(ATTN_HEADS):
        rep_ref[0, h] = jnp.broadcast_to(cum2[:, h:h + 1], (ts, LANES))
    row_ref[0] = cum2.T[0:ATTN_HEADS, :]


def _cum(fl, bf, bsz, seq, ts=512):
    ns = seq // ts
    return pl.pallas_call(
        functools.partial(_cum_kernel, ts=ts),
        grid=(bsz, ns),
        in_specs=[pl.BlockSpec((ts, LANES), lambda b, s: (b * ns + s, 0)),
                  pl.BlockSpec((1, LANES), lambda b, s: (0, 0))],
        out_specs=[pl.BlockSpec((1, ATTN_HEADS, ts, LANES), lambda b, s: (b, 0, s, 0)),
                   pl.BlockSpec((1, ATTN_HEADS, ts), lambda b, s: (b, 0, s))],
        out_shape=[jax.ShapeDtypeStruct((bsz, ATTN_HEADS, seq, LANES), F32),
                   jax.ShapeDtypeStruct((bsz, ATTN_HEADS, seq), F32)],
        scratch_shapes=[pltpu.VMEM((1, LANES), F32)],
        compiler_params=_params(("arbitrary", "arbitrary")),
        name="cum",
    )(fl, bf)


def _attn_kernel(q_ref, k_ref, vt_ref, cq_ref, ck_ref, o_ref, m_ref, l_ref, acc_ref, *, tq):
    qi = pl.program_id(2)
    q = q_ref[...]
    cq = cq_ref[0, 0, 0]
    m_ref[...] = jnp.full_like(m_ref, NEG_BIG)
    l_ref[...] = jnp.zeros_like(l_ref)
    acc_ref[...] = jnp.zeros_like(acc_ref)
    reps = tq // LANES

    def step(j, nblk, causal):
        tk = nblk * tq
        ks = pl.multiple_of(j * tq, tq)
        k = k_ref[pl.ds(ks, tk), :]
        ck = ck_ref[0, 0, pl.ds(ks, tk), :]
        s = lax.dot_general(k, q, _NT, preferred_element_type=F32)
        s = s + (cq - jnp.concatenate([ck] * reps, axis=1))
        if causal:
            key = lax.broadcasted_iota(jnp.int32, (tk, tq), 0) - (nblk - 1) * tq
            qry = lax.broadcasted_iota(jnp.int32, (tk, tq), 1)
            s = jnp.where(key <= qry, s, NEG_BIG)
        m_prev = m_ref[...]
        m_new = jnp.maximum(m_prev, jnp.max(s, axis=0, keepdims=True))
        alpha = jnp.exp2(m_prev - m_new)
        p = jnp.exp2(s - m_new)
        l_ref[...] = alpha * l_ref[...] + jnp.sum(p, axis=0, keepdims=True)
        pb = p.astype(BF16)
        pv = jnp.dot(vt_ref[j], pb[0:tq], preferred_element_type=F32)
        for b in range(1, nblk):
            pv = pv + jnp.dot(vt_ref[j + b], pb[b * tq:(b + 1) * tq], preferred_element_type=F32)
        acc_ref[...] = alpha * acc_ref[...] + pv
        m_ref[...] = m_new

    def quad(i, carry):
        step(4 * i, 4, False)
        return carry

    lax.fori_loop(0, qi // 4, quad, 0)
    rest = 4 * (qi // 4)
    for size in range(1, 5):
        @pl.when(qi - rest + 1 == size)
        def _():
            step(rest, size, True)

    o_ref[...] = (acc_ref[...] / l_ref[...]).T


ATTN_TQ = 512


def _attn(qk, vt, cum_rep, cum_row, bsz, seq, tq=ATTN_TQ):
    t = qk.shape[0]
    nq = seq // tq
    h = ATTN_HEADS
    cq = cum_row.reshape(bsz, h, nq, 1, tq)
    return pl.pallas_call(
        functools.partial(_attn_kernel, tq=tq),
        grid=(bsz, h, nq),
        in_specs=[pl.BlockSpec((tq, HEAD_DIM), lambda b, hh, i: (b * nq + i, hh)),
                  pl.BlockSpec((seq, HEAD_DIM), lambda b, hh, i: (b, h + hh)),
                  pl.BlockSpec((nq, HEAD_DIM, tq), lambda b, hh, i: (b, hh, 0)),
                  pl.BlockSpec((1, 1, 1, 1, tq), lambda b, hh, i: (b, hh, i, 0, 0)),
                  pl.BlockSpec((1, 1, seq, LANES), lambda b, hh, i: (b, hh, 0, 0))],
        out_specs=pl.BlockSpec((tq, HEAD_DIM), lambda b, hh, i: (b * nq + i, hh)),
        out_shape=jax.ShapeDtypeStruct((t, ATTN_WIDTH), F32),
        scratch_shapes=[pltpu.VMEM((1, tq), F32), pltpu.VMEM((1, tq), F32),
                        pltpu.VMEM((HEAD_DIM, tq), F32)],
        compiler_params=_params(("arbitrary", "arbitrary", "arbitrary")),
        name="attn",
    )(qk, qk, vt, cq, cum_rep)


def _lru_kernel(lx_ref, ly_ref, cw_ref, cb_ref, wa_ref, ba_ref, wx_ref, bx_ref, lam_ref,
                g_ref, o_ref, xbuf, hc, *, ts):
    @pl.when(pl.program_id(1) == 0)
    def _():
        xbuf[0:SUBLANES, :] = jnp.zeros((SUBLANES, LRU_WIDTH), F32)
        hc[...] = jnp.zeros_like(hc)

    xbuf[SUBLANES:, :] = lx_ref[...]
    xc = cb_ref[...]
    for k in range(CONV_WIDTH):
        off = SUBLANES - (CONV_WIDTH - 1) + k
        xc = xc + cw_ref[k:k + 1, :] * xbuf[off:off + ts, :]
    xbuf[0:SUBLANES, :] = xbuf[ts:ts + SUBLANES, :]

    xcb = xc.astype(BF16)
    ra, rx = [], []
    for n in range(LRU_BLOCKS):
        xs = xcb[:, n * LRU_BLOCK_DIM:(n + 1) * LRU_BLOCK_DIM]
        ra.append(jnp.dot(xs, wa_ref[n], preferred_element_type=F32))
        rx.append(jnp.dot(xs, wx_ref[n], preferred_element_type=F32))
    r = jax.nn.sigmoid(jnp.concatenate(ra, axis=1) + ba_ref[...])
    ig = jax.nn.sigmoid(jnp.concatenate(rx, axis=1) + bx_ref[...])
    log_a = (-LRU_C) * r * _softplus(-lam_ref[...])
    a = jnp.exp(log_a)
    b = jnp.sqrt(-jnp.tanh(log_a) * (a * a + 1.0)) * (ig * xc)

    rows = lax.broadcasted_iota(jnp.int32, a.shape, 0)
    s = 1
    while s < ts:
        keep = rows >= s
        b = jnp.where(keep, a * pltpu.roll(b, s, axis=0) + b, b)
        a = jnp.where(keep, a * pltpu.roll(a, s, axis=0), a)
        s *= 2
    hseq = b + a * hc[...]
    hc[...] = hseq[ts - 1:ts, :]

    lru = hseq * _gelu_tanh(ly_ref[...])
    o_ref[...] = (_rms(lru) * g_ref[...]).astype(o_ref.dtype)


def _lru(lxy, conv_w, conv_b, w_a, b_a, w_x, b_x, lam, g, bsz, seq, ts=256):
    t = lxy.shape[0]
    ns = seq // ts
    w = LRU_WIDTH
    vec = lambda: pl.BlockSpec((1, w), lambda b, s: (0, 0))
    blk = lambda: pl.BlockSpec((LRU_BLOCKS, LRU_BLOCK_DIM, LRU_BLOCK_DIM), lambda b, s: (0, 0, 0))
    return pl.pallas_call(
        functools.partial(_lru_kernel, ts=ts),
        grid=(bsz, ns),
        in_specs=[pl.BlockSpec((ts, w), lambda b, s: (b * ns + s, 0)),
                  pl.BlockSpec((ts, w), lambda b, s: (b * ns + s, 1)),
                  pl.BlockSpec((CONV_WIDTH, w), lambda b, s: (0, 0)),
                  vec(), blk(), vec(), blk(), vec(), vec(), vec()],
        out_specs=pl.BlockSpec((ts, w), lambda b, s: (b * ns + s, 0)),
        out_shape=jax.ShapeDtypeStruct((t, w), BF16),
        scratch_shapes=[pltpu.VMEM((ts + SUBLANES, w), F32), pltpu.VMEM((1, w), F32)],
        compiler_params=_params(("arbitrary", "arbitrary")),
        name="lru",
    )(lxy, lxy, conv_w, conv_b.reshape(1, w), w_a.astype(BF16), b_a.reshape(1, w),
      w_x.astype(BF16), b_x.reshape(1, w), lam.reshape(1, w), g.reshape(1, w))


def _outproj_kernel(at_ref, lr_ref, x_ref, ga_ref, woa_ref, wol_ref, g1_ref, sc_ref, sh_ref,
                    gf_ref, wq_ref, x1_ref, h2t_ref, qp_ref):
    an = (_rms(at_ref[...]) * ga_ref[...]).astype(BF16)
    mix = jnp.dot(an, woa_ref[...], preferred_element_type=F32)
    mix = mix + jnp.dot(lr_ref[...], wol_ref[...], preferred_element_type=F32)
    x1 = x_ref[...] + g1_ref[0] * mix
    x1_ref[...] = x1
    h2 = (_rms(x1) * gf_ref[...]) * (1.0 + sc_ref[0]) + sh_ref[0]
    h2t_ref[...] = h2.T.astype(h2t_ref.dtype)
    qp = jnp.dot(h2.astype(BF16), wq_ref[...], preferred_element_type=F32).astype(qp_ref.dtype)
    width = qp_ref.shape[2]
    for h in range(qp_ref.shape[0]):
        qp_ref[h] = qp[:, h * width:(h + 1) * width]


def _outproj(attn, lru_n, x2, g_attn, w_oa, w_ol, ga1, sc2, sh2, g_ffn, w_q, nh, seq, tm=256):
    t, d = x2.shape
    nb = seq // tm
    aw = attn.shape[1]
    lw = lru_n.shape[1]
    nq = w_q.shape[1]
    mod = lambda: pl.BlockSpec((1, 1, d), lambda i: (i // nb, 0, 0))
    return pl.pallas_call(
        _outproj_kernel,
        grid=(t // tm,),
        in_specs=[pl.BlockSpec((tm, aw), lambda i: (i, 0)),
                  pl.BlockSpec((tm, lw), lambda i: (i, 0)),
                  pl.BlockSpec((tm, d), lambda i: (i, 0)),
                  pl.BlockSpec((1, aw), lambda i: (0, 0)),
                  pl.BlockSpec((aw, d), lambda i: (0, 0)),
                  pl.BlockSpec((lw, d), lambda i: (0, 0)),
                  mod(), mod(), mod(),
                  pl.BlockSpec((1, d), lambda i: (0, 0)),
                  pl.BlockSpec((d, nq), lambda i: (0, 0))],
        out_specs=[pl.BlockSpec((tm, d), lambda i: (i, 0)),
                   pl.BlockSpec((d, tm), lambda i: (0, i)),
                   pl.BlockSpec((nh, tm, nq // nh), lambda i: (0, i, 0))],
        out_shape=[jax.ShapeDtypeStruct((t, d), F32), jax.ShapeDtypeStruct((d, t), BF16),
                   jax.ShapeDtypeStruct((nh, t, nq // nh), BF16)],
        compiler_params=_params(("arbitrary",)),
        name="outproj",
    )(attn, lru_n, x2, g_attn.reshape(1, aw), w_oa, w_ol, ga1, sc2, sh2, g_ffn.reshape(1, d), w_q)


def _sort16_pairs():
    pairs = []

    def merge(lo, hi, r):
        step = r * 2
        if step < hi - lo:
            merge(lo, hi, step)
            merge(lo + r, hi, step)
            for i in range(lo + r, hi - r, step):
                pairs.append((i, i + r))
        else:
            pairs.append((lo, lo + r))

    def sort(lo, hi):
        if hi - lo >= 1:
            mid = lo + (hi - lo) // 2
            sort(lo, mid)
            sort(mid + 1, hi)
            merge(lo, hi, 1)

    sort(0, PEER_TOPK - 1)
    return pairs


_SORT16 = _sort16_pairs()


def _cex(v, i, j):
    hi, lo = jnp.maximum(v[i], v[j]), jnp.minimum(v[i], v[j])
    v[i], v[j] = hi, lo


def _sort16(v):
    v = list(v)
    for i, j in _SORT16:
        _cex(v, i, j)
    return v


def _bitonic_merge16(v):
    v = list(v)
    d = PEER_TOPK // 2
    while d >= 1:
        for i in range(PEER_TOPK):
            if i & d == 0:
                _cex(v, i, i + d)
        d //= 2
    return v


def _top16_of_union(a, b):
    return [jnp.maximum(a[i], b[PEER_TOPK - 1 - i]) for i in range(PEER_TOPK)]


def _top16_sorted(s):
    v = _sort16([s[i * SUBLANES:(i + 1) * SUBLANES, :] for i in range(PEER_TOPK)])
    for shift in (4, 2, 1):
        other = [pltpu.roll(x, shift, axis=0) for x in v]
        v = _bitonic_merge16(_top16_of_union(v, other))
    return v


def _threshold_stats(v1, v2):
    k = PEER_TOPK
    cells = {(i, j): v1[i] + v2[j] for i in range(k) for j in range(k) if (i + 1) * (j + 1) <= k}
    row0 = [cells[0, j] for j in range(k)]
    col0 = [cells[i, 0] for i in range(1, k)]
    rest = ([(1, j) for j in range(1, 8)] + [(i, 1) for i in range(2, 8)]
            + [(2, j) for j in range(2, 5)] + [(i, 2) for i in range(3, 5)] + [(3, 3)])
    rest = [cells[ij] for ij in rest]
    ab = [row0[0]] + [jnp.maximum(row0[i], col0[k - 1 - i]) for i in range(1, k)]
    ab = _bitonic_merge16(ab)
    c1 = _sort16(rest[:k])
    t = _bitonic_merge16(_top16_of_union(ab, c1))
    y = list(rest[k:])
    _cex(y, 0, 1), _cex(y, 1, 2), _cex(y, 0, 1)
    thr = jnp.minimum(jnp.minimum(t[12], jnp.maximum(t[13], y[2])),
                      jnp.minimum(jnp.maximum(t[14], y[1]), jnp.maximum(t[15], y[0])))
    top = row0[0]
    z = jnp.zeros_like(top)
    for c in cells.values():
        z = z + jnp.where(c >= thr, jnp.exp(c - top), 0.0)
    t = []
    for j in range(k):
        tj = jnp.full_like(top, jnp.inf)
        for i in range(k // (j + 1)):
            tj = jnp.where(cells[i, j] >= thr, v1[i], tj)
        t.append(tj)
    return z, t


def _topk_kernel(qp_ref, k1_ref, k2_ref, c1_ref, a1_ref, s2_ref, b2_ref):
    reps = PEER_KEYS // SUBLANES

    def head(h):
        q = qp_ref[h]
        s1 = lax.dot_general(k1_ref[h], q[:, :PEER_HALF], _NT, preferred_element_type=F32)
        s2 = lax.dot_general(k2_ref[h], q[:, PEER_HALF:], _NT, preferred_element_type=F32)
        v1 = _top16_sorted(s1)
        v2 = _top16_sorted(s2)
        z, t = _threshold_stats(v1, v2)
        c1 = jnp.full_like(s1, jnp.inf)
        for tj, v in zip(t, v2):
            c1 = jnp.where(s1 >= jnp.tile(tj, (reps, 1)), jnp.tile(v, (reps, 1)), c1)
        c1_ref[0, h] = c1
        s2_ref[0, h] = s2
        a1_ref[0, h] = jnp.exp(s1 - v1[0][0:1, :]) / z[0:1, :]
        b2_ref[0, h] = jnp.exp(s2 - v2[0][0:1, :])

    def two_heads(i, carry):
        head(2 * i)
        head(2 * i + 1)
        return carry

    lax.fori_loop(0, PEER_HEADS // 2, two_heads, 0)


def _topk(qp3, k1, k2):
    nh, t, qd = qp3.shape
    nc = t // LANES
    big = lambda: pl.BlockSpec((1, nh, PEER_KEYS, LANES), lambda i: (i, 0, 0, 0))
    keys = lambda: pl.BlockSpec((nh, PEER_KEYS, PEER_HALF), lambda i: (0, 0, 0))
    sds = jax.ShapeDtypeStruct((nc, nh, PEER_KEYS, LANES), F32)
    return pl.pallas_call(
        _topk_kernel,
        grid=(nc,),
        in_specs=[pl.BlockSpec((nh, LANES, qd), lambda i: (0, i, 0)), keys(), keys()],
        out_specs=[big(), big(), big(), big()],
        out_shape=[sds, sds, sds, sds],
        compiler_params=_params(("arbitrary",)),
        name="topk",
    )(qp3, k1, k2)


PEER_TILE = SUBLANES * PEER_KEYS


def _peer_kernel(ht_ref, u_ref, vt_ref, c1_ref, a1_ref, s2_ref, b2_ref, x1_ref, g2_ref, gf_ref,
                 o_ref, acc, st, act, *, nc):
    j = pl.program_id(1)

    @pl.when(j == 0)
    def _():
        acc[...] = jnp.zeros_like(acc)

    @pl.when(j >= 0)
    def _():
        st[...] = jnp.dot(u_ref[...], ht_ref[...], preferred_element_type=F32)

    @pl.when(j >= -1)
    def _():
        for c in range(nc):
            cs = slice(c * LANES, (c + 1) * LANES)
            for r in range(SUBLANES):
                w = None
                for h in range(PEER_HEADS):
                    sel = jnp.where(s2_ref[c, h] >= c1_ref[c, h, r:r + 1, :], b2_ref[c, h], 0.0)
                    term = sel * a1_ref[c, h, r:r + 1, :]
                    w = term if w is None else w + term
                rows = slice(r * PEER_KEYS, (r + 1) * PEER_KEYS)
                act[rows, cs] = (_gelu_tanh(st[rows, cs]) * w).astype(BF16)

    @pl.when(j >= -2)
    def _():
        acc[...] += jnp.dot(vt_ref[0], act[...], preferred_element_type=F32)

    @pl.when(j == pl.num_programs(1) - 1)
    def _():
        o_ref[...] = _rms(x1_ref[...] + g2_ref[0] * acc[...].T) * gf_ref[...]


def _peer(h2t, u_b, v_tiles, c1, a1, s2, b2, x1, ga2, g_final, seq, tm=512):
    d, t = h2t.shape
    te = PEER_TILE
    ne = u_b.shape[0] // te
    nh = c1.shape[1]
    nc = tm // LANES
    nb = seq // tm
    rows = lambda: pl.BlockSpec((nc, nh, SUBLANES, LANES), lambda i, j: (i, 0, j, 0))
    full = lambda: pl.BlockSpec((nc, nh, PEER_KEYS, LANES), lambda i, j: (i, 0, 0, 0),
                                pipeline_mode=pl.Buffered(1))
    return pl.pallas_call(
        functools.partial(_peer_kernel, nc=nc),
        grid=(t // tm, ne),
        in_specs=[pl.BlockSpec((d, tm), lambda i, j: (0, i)),
                  pl.BlockSpec((te, d), lambda i, j: (j, 0)),
                  pl.BlockSpec((1, d, te), lambda i, j: (j, 0, 0)),
                  rows(), rows(), full(), full(),
                  pl.BlockSpec((tm, d), lambda i, j: (i, 0)),
                  pl.BlockSpec((1, 1, d), lambda i, j: (i // nb, 0, 0)),
                  pl.BlockSpec((1, d), lambda i, j: (0, 0))],
        out_specs=pl.BlockSpec((tm, d), lambda i, j: (i, 0)),
        out_shape=jax.ShapeDtypeStruct((t, d), F32),
        scratch_shapes=[pltpu.VMEM((d, tm), F32), pltpu.VMEM((te, tm), F32),
                        pltpu.VMEM((te, tm), BF16)],
        compiler_params=_params(("arbitrary", "arbitrary")),
        name="peer",
    )(h2t, u_b, v_tiles, c1, a1, s2, b2, x1, ga2, g_final.reshape(1, d))


def kernel(x, c, w_ada, b_ada, g_mix, w_in, b_f, conv_w, conv_b, lru_w_a, lru_b_a, lru_w_x,
           lru_b_x, lru_lambda, g_attn_out, g_lru_out, w_out, g_ffn, peer_w_q, peer_k1,
           peer_k2, peer_u, peer_v, g_final):
    bsz, seq, d = x.shape
    depth = w_ada.shape[0]
    assert depth == 1, "the final residual + norm is fused into the PEER kernel of a single layer"
    x2 = x.reshape(bsz * seq, d)
    off_f = 3 * ATTN_WIDTH
    off_l = off_f + ATTN_HEADS

    for l in range(depth):
        mod = _ada(c, w_ada[l], b_ada[l])
        sh1, sc1, ga1, sh2, sc2, ga2 = [m.reshape(bsz, 1, d) for m in jnp.split(mod, N_ADA, axis=-1)]

        h = _modnorm(x2, g_mix[l], sc1, sh1, seq)
        w_l = w_in[l]
        q_scale = jnp.where(jnp.arange(2 * ATTN_WIDTH) < ATTN_WIDTH, LOG2E * HEAD_DIM ** -0.5, 1.0)
        qk = _matmul_colscale(h, w_l[:, :2 * ATTN_WIDTH].astype(BF16), q_scale.astype(F32), BF16,
                              name="proj_qk")
        vt = _matmul_t(h, w_l[:, 2 * ATTN_WIDTH:off_f].T.astype(BF16), BF16, tm=ATTN_TQ,
                       name="proj_vt")
        w_f = jnp.pad(w_l[:, off_f:off_l], ((0, 0), (0, LANES - ATTN_HEADS))).astype(BF16)
        fl = _matmul(h, w_f, F32, name="proj_f")
        lxy = _matmul(h, w_l[:, off_l:].astype(BF16), F32, name="proj_lru")

        bf = jnp.pad(b_f[l], (0, LANES - ATTN_HEADS)).reshape(1, LANES)
        cum_rep, cum_row = _cum(fl, bf, bsz, seq)
        attn = _attn(qk, vt, cum_rep, cum_row, bsz, seq)
        lru_n = _lru(lxy, conv_w[l], conv_b[l], lru_w_a[l], lru_b_a[l], lru_w_x[l], lru_b_x[l],
                     lru_lambda[l], g_lru_out[l], bsz, seq)

        w_o = w_out[l].astype(BF16)
        x1, h2t, qp3 = _outproj(attn, lru_n, x2, g_attn_out[l], w_o[:ATTN_WIDTH],
                                w_o[ATTN_WIDTH:], ga1, sc2, sh2, g_ffn[l],
                                peer_w_q[l].astype(BF16), PEER_HEADS, seq)
        c1, a1, s2, b2 = _topk(qp3, peer_k1[l].astype(BF16), peer_k2[l].astype(BF16))
        v_tiles = peer_v[l].reshape(-1, PEER_TILE, d).transpose(0, 2, 1).astype(BF16)
        out = _peer(h2t, peer_u[l].astype(BF16), v_tiles, c1, a1, s2, b2, x1, ga2, g_final, seq)
    return out.reshape(bsz, seq, d)
```

```python
import functools

import jax
import jax.numpy as jnp
from jax import lax
from jax.experimental import pallas as pl
from jax.experimental.pallas import tpu as pltpu

F32 = jnp.float32
BF16 = jnp.bfloat16

ATTN_HEADS = 8
HEAD_DIM = 128
ATTN_WIDTH = ATTN_HEADS * HEAD_DIM
LRU_BLOCKS = 8
LRU_BLOCK_DIM = 128
LRU_WIDTH = LRU_BLOCKS * LRU_BLOCK_DIM
CONV_WIDTH = 4
LRU_C = 8.0
PEER_HEADS = 8
PEER_KEYS = 128
PEER_HALF = 128
PEER_TOPK = 16
N_ADA = 6
EPS = 1e-6
LANES = 128
SUBLANES = 8
NEG_BIG = -1e30
LOG2E = 1.4426950408889634
VMEM_LIMIT = 56 * 1024 * 1024

_NT = (((1,), (1,)), ((), ()))


def _params(sem, vmem=VMEM_LIMIT):
    return pltpu.CompilerParams(dimension_semantics=sem, vmem_limit_bytes=vmem)


def _gelu_tanh(x):
    k1 = -2.0 * 0.7978845608028654 * LOG2E
    k3 = k1 * 0.044715
    return x / (1.0 + jnp.exp2(x * (k1 + k3 * (x * x))))


def _softplus(z):
    return jnp.maximum(z, 0.0) + jnp.log1p(jnp.exp(-jnp.abs(z)))


def _rms(x):
    return x * lax.rsqrt(jnp.mean(x * x, axis=-1, keepdims=True) + EPS)


def _ada_kernel(c_ref, w_ref, b_ref, o_ref):
    @pl.when(pl.program_id(0) == 0)
    def _():
        o_ref[...] = jnp.broadcast_to(b_ref[...], o_ref.shape)

    c = c_ref[...]
    sc = c * jax.nn.sigmoid(c)
    o_ref[...] += jnp.dot(sc.astype(BF16), w_ref[...].astype(BF16), preferred_element_type=F32)


def _ada(c, w, b, tk=256):
    bsz, d = c.shape
    n = w.shape[1]
    return pl.pallas_call(
        _ada_kernel,
        grid=(d // tk,),
        in_specs=[pl.BlockSpec((bsz, tk), lambda k: (0, k)),
                  pl.BlockSpec((tk, n), lambda k: (k, 0)),
                  pl.BlockSpec((1, n), lambda k: (0, 0))],
        out_specs=pl.BlockSpec((bsz, n), lambda k: (0, 0)),
        out_shape=jax.ShapeDtypeStruct((bsz, n), F32),
        compiler_params=_params(("arbitrary",)),
        name="ada",
    )(c, w, b.reshape(1, n))


def _modnorm_kernel(x_ref, g_ref, sc_ref, sh_ref, o_ref):
    y = _rms(x_ref[...]) * g_ref[...]
    o_ref[...] = (y * (1.0 + sc_ref[0]) + sh_ref[0]).astype(o_ref.dtype)


def _modnorm(x2, g, sc, sh, seq, tm=512):
    t, d = x2.shape
    nb = seq // tm
    return pl.pallas_call(
        _modnorm_kernel,
        grid=(t // tm,),
        in_specs=[pl.BlockSpec((tm, d), lambda i: (i, 0)),
                  pl.BlockSpec((1, d), lambda i: (0, 0)),
                  pl.BlockSpec((1, 1, d), lambda i: (i // nb, 0, 0)),
                  pl.BlockSpec((1, 1, d), lambda i: (i // nb, 0, 0))],
        out_specs=pl.BlockSpec((tm, d), lambda i: (i, 0)),
        out_shape=jax.ShapeDtypeStruct((t, d), BF16),
        compiler_params=_params(("arbitrary",)),
        name="modnorm",
    )(x2, g.reshape(1, d), sc, sh)


def _matmul_kernel(a_ref, w_ref, o_ref):
    o_ref[...] = jnp.dot(a_ref[...], w_ref[...],
                         preferred_element_type=F32).astype(o_ref.dtype)


def _matmul(a, w, out_dtype, tm=512, tn=1024, name="matmul"):
    t, k = a.shape
    n = w.shape[1]
    tn = min(tn, n)
    return pl.pallas_call(
        _matmul_kernel,
        grid=(n // tn, t // tm),
        in_specs=[pl.BlockSpec((tm, k), lambda j, i: (i, 0)),
                  pl.BlockSpec((k, tn), lambda j, i: (0, j))],
        out_specs=pl.BlockSpec((tm, tn), lambda j, i: (i, j)),
        out_shape=jax.ShapeDtypeStruct((t, n), out_dtype),
        compiler_params=_params(("arbitrary", "arbitrary")),
        name=name,
    )(a, w)


def _matmul_colscale_kernel(a_ref, w_ref, cs_ref, o_ref):
    acc = jnp.dot(a_ref[...], w_ref[...], preferred_element_type=F32)
    o_ref[...] = (acc * cs_ref[...]).astype(o_ref.dtype)


def _matmul_colscale(a, w, col_scale, out_dtype, tm=512, tn=1024, name="matmul_colscale"):
    t, k = a.shape
    n = w.shape[1]
    return pl.pallas_call(
        _matmul_colscale_kernel,
        grid=(n // tn, t // tm),
        in_specs=[pl.BlockSpec((tm, k), lambda j, i: (i, 0)),
                  pl.BlockSpec((k, tn), lambda j, i: (0, j)),
                  pl.BlockSpec((1, tn), lambda j, i: (0, j))],
        out_specs=pl.BlockSpec((tm, tn), lambda j, i: (i, j)),
        out_shape=jax.ShapeDtypeStruct((t, n), out_dtype),
        compiler_params=_params(("arbitrary", "arbitrary")),
        name=name,
    )(a, w, col_scale.reshape(1, n))


def _matmul_t_kernel(a_ref, wt_ref, o_ref):
    o_ref[0] = lax.dot_general(wt_ref[...], a_ref[...], _NT,
                               preferred_element_type=F32).astype(o_ref.dtype)


def _matmul_t(a, wt, out_dtype, tm=512, tn=1024, name="matmul_t"):
    t, k = a.shape
    n = wt.shape[0]
    tn = min(tn, n)
    return pl.pallas_call(
        _matmul_t_kernel,
        grid=(n // tn, t // tm),
        in_specs=[pl.BlockSpec((tm, k), lambda j, i: (i, 0)),
                  pl.BlockSpec((tn, k), lambda j, i: (j, 0))],
        out_specs=pl.BlockSpec((1, tn, tm), lambda j, i: (i, j, 0)),
        out_shape=jax.ShapeDtypeStruct((t // tm, n, tm), out_dtype),
        compiler_params=_params(("arbitrary", "arbitrary")),
        name=name,
    )(a, wt)


def _cum_kernel(fl_ref, bf_ref, rep_ref, row_ref, carry_ref, *, ts):
    @pl.when(pl.program_id(1) == 0)
    def _():
        carry_ref[...] = jnp.zeros_like(carry_ref)

    lf = -_softplus(-(fl_ref[...] + bf_ref[...]))
    rows = lax.broadcasted_iota(jnp.int32, lf.shape, 0)
    s = 1
    while s < ts:
        lf = lf + jnp.where(rows >= s, pltpu.roll(lf, s, axis=0), 0.0)
        s *= 2
    cum = lf + carry_ref[...]
    carry_ref[...] = cum[ts - 1:ts, :]
    cum2 = cum * LOG2E
    for h in range(ATTN_HEADS):
        rep_ref[0, h] = jnp.broadcast_to(cum2[:, h:h + 1], (ts, LANES))
    row_ref[0] = cum2.T[0:ATTN_HEADS, :]


def _cum(fl, col, bf, bsz, seq, ts=512):
    ns = seq // ts
    return pl.pallas_call(
        functools.partial(_cum_kernel, ts=ts),
        grid=(bsz, ns),
        in_specs=[pl.BlockSpec((ts, LANES), lambda b, s: (b * ns + s, col)),
                  pl.BlockSpec((1, LANES), lambda b, s: (0, 0))],
        out_specs=[pl.BlockSpec((1, ATTN_HEADS, ts, LANES), lambda b, s: (b, 0, s, 0)),
                   pl.BlockSpec((1, ATTN_HEADS, ts), lambda b, s: (b, 0, s))],
        out_shape=[jax.ShapeDtypeStruct((bsz, ATTN_HEADS, seq, LANES), F32),
                   jax.ShapeDtypeStruct((bsz, ATTN_HEADS, seq), F32)],
        scratch_shapes=[pltpu.VMEM((1, LANES), F32)],
        compiler_params=_params(("arbitrary", "arbitrary")),
        name="cum",
    )(fl, bf)


def _attn_kernel(q_ref, k_ref, vt_ref, cq_ref, ck_ref, o_ref, m_ref, l_ref, acc_ref, *, tq):
    qi = pl.program_id(2)
    q = q_ref[...]
    cq = cq_ref[0, 0, 0]
    m_ref[...] = jnp.full_like(m_ref, NEG_BIG)
    l_ref[...] = jnp.zeros_like(l_ref)
    acc_ref[...] = jnp.zeros_like(acc_ref)
    reps = tq // LANES

    def step(j, nblk, causal):
        tk = nblk * tq
        ks = pl.multiple_of(j * tq, tq)
        k = k_ref[pl.ds(ks, tk), :]
        ck = ck_ref[0, 0, pl.ds(ks, tk), :]
        s = lax.dot_general(k, q, _NT, preferred_element_type=F32)
        s = s + (cq - jnp.concatenate([ck] * reps, axis=1))
        if causal:
            key = lax.broadcasted_iota(jnp.int32, (tk, tq), 0) - (nblk - 1) * tq
            qry = lax.broadcasted_iota(jnp.int32, (tk, tq), 1)
            s = jnp.where(key <= qry, s, NEG_BIG)
        m_prev = m_ref[...]
        m_new = jnp.maximum(m_prev, jnp.max(s, axis=0, keepdims=True))
        alpha = jnp.exp2(m_prev - m_new)
        p = jnp.exp2(s - m_new)
        l_ref[...] = alpha * l_ref[...] + jnp.sum(p, axis=0, keepdims=True)
        pb = p.astype(BF16)
        pv = jnp.dot(vt_ref[j], pb[0:tq], preferred_element_type=F32)
        for b in range(1, nblk):
            pv = pv + jnp.dot(vt_ref[j + b], pb[b * tq:(b + 1) * tq], preferred_element_type=F32)
        acc_ref[...] = alpha * acc_ref[...] + pv
        m_ref[...] = m_new

    grp = ATTN_KEYS_PER_STEP // tq

    def group(i, carry):
        step(grp * i, grp, False)
        return carry

    lax.fori_loop(0, qi // grp, group, 0)
    rest = grp * (qi // grp)
    for size in range(1, grp + 1):
        @pl.when(qi - rest + 1 == size)
        def _():
            step(rest, size, True)

    o_ref[...] = (acc_ref[...] / l_ref[...]).T


ATTN_TQ = 512
ATTN_KEYS_PER_STEP = 2048


def _attn(qk, vt, cum_rep, cum_row, bsz, seq, tq=ATTN_TQ):
    t = qk.shape[0]
    nq = seq // tq
    h = ATTN_HEADS
    cq = cum_row.reshape(bsz, h, nq, 1, tq)
    return pl.pallas_call(
        functools.partial(_attn_kernel, tq=tq),
        grid=(bsz, h, nq),
        in_specs=[pl.BlockSpec((tq, HEAD_DIM), lambda b, hh, i: (b * nq + i, hh)),
                  pl.BlockSpec((seq, HEAD_DIM), lambda b, hh, i: (b, h + hh)),
                  pl.BlockSpec((nq, HEAD_DIM, tq), lambda b, hh, i: (b, hh, 0)),
                  pl.BlockSpec((1, 1, 1, 1, tq), lambda b, hh, i: (b, hh, i, 0, 0)),
                  pl.BlockSpec((1, 1, seq, LANES), lambda b, hh, i: (b, hh, 0, 0))],
        out_specs=pl.BlockSpec((tq, HEAD_DIM), lambda b, hh, i: (b * nq + i, hh)),
        out_shape=jax.ShapeDtypeStruct((t, ATTN_WIDTH), F32),
        scratch_shapes=[pltpu.VMEM((1, tq), F32), pltpu.VMEM((1, tq), F32),
                        pltpu.VMEM((HEAD_DIM, tq), F32)],
        compiler_params=_params(("arbitrary", "arbitrary", "arbitrary")),
        name="attn",
    )(qk, qk, vt, cq, cum_rep)


def _lru_kernel(lx_ref, ly_ref, cw_ref, cb_ref, wa_ref, ba_ref, wx_ref, bx_ref, lam_ref,
                g_ref, o_ref, xbuf, hc, *, ts):
    @pl.when(pl.program_id(1) == 0)
    def _():
        xbuf[0:SUBLANES, :] = jnp.zeros((SUBLANES, LRU_WIDTH), F32)
        hc[...] = jnp.zeros_like(hc)

    xbuf[SUBLANES:, :] = lx_ref[...]
    xc = cb_ref[...]
    for k in range(CONV_WIDTH):
        off = SUBLANES - (CONV_WIDTH - 1) + k
        xc = xc + cw_ref[k:k + 1, :] * xbuf[off:off + ts, :]
    xbuf[0:SUBLANES, :] = xbuf[ts:ts + SUBLANES, :]

    xcb = xc.astype(BF16)
    ra, rx = [], []
    for n in range(LRU_BLOCKS):
        xs = xcb[:, n * LRU_BLOCK_DIM:(n + 1) * LRU_BLOCK_DIM]
        ra.append(jnp.dot(xs, wa_ref[n], preferred_element_type=F32))
        rx.append(jnp.dot(xs, wx_ref[n], preferred_element_type=F32))
    r = jax.nn.sigmoid(jnp.concatenate(ra, axis=1) + ba_ref[...])
    ig = jax.nn.sigmoid(jnp.concatenate(rx, axis=1) + bx_ref[...])
    log_a = (-LRU_C) * r * _softplus(-lam_ref[...])
    a = jnp.exp(log_a)
    b = jnp.sqrt(-jnp.tanh(log_a) * (a * a + 1.0)) * (ig * xc)

    rows = lax.broadcasted_iota(jnp.int32, a.shape, 0)
    s = 1
    while s < ts:
        keep = rows >= s
        b = jnp.where(keep, a * pltpu.roll(b, s, axis=0) + b, b)
        a = jnp.where(keep, a * pltpu.roll(a, s, axis=0), a)
        s *= 2
    hseq = b + a * hc[...]
    hc[...] = hseq[ts - 1:ts, :]

    lru = hseq * _gelu_tanh(ly_ref[...])
    o_ref[...] = (_rms(lru) * g_ref[...]).astype(o_ref.dtype)


def _lru(lxy, conv_w, conv_b, w_a, b_a, w_x, b_x, lam, g, bsz, seq, ts=256):
    t = lxy.shape[0]
    ns = seq // ts
    w = LRU_WIDTH
    vec = lambda: pl.BlockSpec((1, w), lambda b, s: (0, 0))
    blk = lambda: pl.BlockSpec((LRU_BLOCKS, LRU_BLOCK_DIM, LRU_BLOCK_DIM), lambda b, s: (0, 0, 0))
    return pl.pallas_call(
        functools.partial(_lru_kernel, ts=ts),
        grid=(bsz, ns),
        in_specs=[pl.BlockSpec((ts, w), lambda b, s: (b * ns + s, 0)),
                  pl.BlockSpec((ts, w), lambda b, s: (b * ns + s, 1)),
                  pl.BlockSpec((CONV_WIDTH, w), lambda b, s: (0, 0)),
                  vec(), blk(), vec(), blk(), vec(), vec(), vec()],
        out_specs=pl.BlockSpec((ts, w), lambda b, s: (b * ns + s, 0)),
        out_shape=jax.ShapeDtypeStruct((t, w), BF16),
        scratch_shapes=[pltpu.VMEM((ts + SUBLANES, w), F32), pltpu.VMEM((1, w), F32)],
        compiler_params=_params(("arbitrary", "arbitrary")),
        name="lru",
    )(lxy, lxy, conv_w, conv_b.reshape(1, w), w_a.astype(BF16), b_a.reshape(1, w),
      w_x.astype(BF16), b_x.reshape(1, w), lam.reshape(1, w), g.reshape(1, w))


def _outproj_kernel(at_ref, lr_ref, x_ref, ga_ref, woa_ref, wol_ref, g1_ref, sc_ref, sh_ref,
                    gf_ref, wq_ref, x1_ref, h2t_ref, qp_ref):
    an = (_rms(at_ref[...]) * ga_ref[...]).astype(BF16)
    mix = jnp.dot(an, woa_ref[...], preferred_element_type=F32)
    mix = mix + jnp.dot(lr_ref[...], wol_ref[...], preferred_element_type=F32)
    x1 = x_ref[...] + g1_ref[0] * mix
    x1_ref[...] = x1
    h2 = (_rms(x1) * gf_ref[...]) * (1.0 + sc_ref[0]) + sh_ref[0]
    h2t_ref[...] = h2.T.astype(h2t_ref.dtype)
    qp = jnp.dot(h2.astype(BF16), wq_ref[...], preferred_element_type=F32).astype(qp_ref.dtype)
    width = qp_ref.shape[2]
    for h in range(qp_ref.shape[0]):
        qp_ref[h] = qp[:, h * width:(h + 1) * width]


def _outproj(attn, lru_n, x2, g_attn, w_oa, w_ol, ga1, sc2, sh2, g_ffn, w_q, nh, seq, tm=256):
    t, d = x2.shape
    nb = seq // tm
    aw = attn.shape[1]
    lw = lru_n.shape[1]
    nq = w_q.shape[1]
    mod = lambda: pl.BlockSpec((1, 1, d), lambda i: (i // nb, 0, 0))
    return pl.pallas_call(
        _outproj_kernel,
        grid=(t // tm,),
        in_specs=[pl.BlockSpec((tm, aw), lambda i: (i, 0)),
                  pl.BlockSpec((tm, lw), lambda i: (i, 0)),
                  pl.BlockSpec((tm, d), lambda i: (i, 0)),
                  pl.BlockSpec((1, aw), lambda i: (0, 0)),
                  pl.BlockSpec((aw, d), lambda i: (0, 0)),
                  pl.BlockSpec((lw, d), lambda i: (0, 0)),
                  mod(), mod(), mod(),
                  pl.BlockSpec((1, d), lambda i: (0, 0)),
                  pl.BlockSpec((d, nq), lambda i: (0, 0))],
        out_specs=[pl.BlockSpec((tm, d), lambda i: (i, 0)),
                   pl.BlockSpec((d, tm), lambda i: (0, i)),
                   pl.BlockSpec((nh, tm, nq // nh), lambda i: (0, i, 0))],
        out_shape=[jax.ShapeDtypeStruct((t, d), F32), jax.ShapeDtypeStruct((d, t), BF16),
                   jax.ShapeDtypeStruct((nh, t, nq // nh), BF16)],
        compiler_params=_params(("arbitrary",)),
        name="outproj",
    )(attn, lru_n, x2, g_attn.reshape(1, aw), w_oa, w_ol, ga1, sc2, sh2, g_ffn.reshape(1, d), w_q)


def _sort16_pairs():
    pairs = []

    def merge(lo, hi, r):
        step = r * 2
        if step < hi - lo:
            merge(lo, hi, step)
            merge(lo + r, hi, step)
            for i in range(lo + r, hi - r, step):
                pairs.append((i, i + r))
        else:
            pairs.append((lo, lo + r))

    def sort(lo, hi):
        if hi - lo >= 1:
            mid = lo + (hi - lo) // 2
            sort(lo, mid)
            sort(mid + 1, hi)
            merge(lo, hi, 1)

    sort(0, PEER_TOPK - 1)
    return pairs


_SORT16 = _sort16_pairs()


def _cex(v, i, j):
    hi, lo = jnp.maximum(v[i], v[j]), jnp.minimum(v[i], v[j])
    v[i], v[j] = hi, lo


def _sort16(v):
    v = list(v)
    for i, j in _SORT16:
        _cex(v, i, j)
    return v


def _bitonic_merge16(v):
    v = list(v)
    d = PEER_TOPK // 2
    while d >= 1:
        for i in range(PEER_TOPK):
            if i & d == 0:
                _cex(v, i, i + d)
        d //= 2
    return v


def _top16_of_union(a, b):
    return [jnp.maximum(a[i], b[PEER_TOPK - 1 - i]) for i in range(PEER_TOPK)]


def _top16_sorted(s):
    v = _sort16([s[i * SUBLANES:(i + 1) * SUBLANES, :] for i in range(PEER_TOPK)])
    for shift in (4, 2, 1):
        other = [pltpu.roll(x, shift, axis=0) for x in v]
        v = _bitonic_merge16(_top16_of_union(v, other))
    return v


def _threshold_stats(v1, v2):
    k = PEER_TOPK
    cells = {(i, j): v1[i] + v2[j] for i in range(k) for j in range(k) if (i + 1) * (j + 1) <= k}
    row0 = [cells[0, j] for j in range(k)]
    col0 = [cells[i, 0] for i in range(1, k)]
    rest = ([(1, j) for j in range(1, 8)] + [(i, 1) for i in range(2, 8)]
            + [(2, j) for j in range(2, 5)] + [(i, 2) for i in range(3, 5)] + [(3, 3)])
    rest = [cells[ij] for ij in rest]
    ab = [row0[0]] + [jnp.maximum(row0[i], col0[k - 1 - i]) for i in range(1, k)]
    ab = _bitonic_merge16(ab)
    c1 = _sort16(rest[:k])
    t = _bitonic_merge16(_top16_of_union(ab, c1))
    y = list(rest[k:])
    _cex(y, 0, 1), _cex(y, 1, 2), _cex(y, 0, 1)
    thr = jnp.minimum(jnp.minimum(t[12], jnp.maximum(t[13], y[2])),
                      jnp.minimum(jnp.maximum(t[14], y[1]), jnp.maximum(t[15], y[0])))
    top = row0[0]
    z = jnp.zeros_like(top)
    for c in cells.values():
        z = z + jnp.where(c >= thr, jnp.exp(c - top), 0.0)
    t = []
    for j in range(k):
        tj = jnp.full_like(top, jnp.inf)
        for i in range(k // (j + 1)):
            tj = jnp.where(cells[i, j] >= thr, v1[i], tj)
        t.append(tj)
    return z, t


def _topk_kernel(qp_ref, k1_ref, k2_ref, c1_ref, a1_ref, s2_ref, b2_ref):
    reps = PEER_KEYS // SUBLANES

    def head(h):
        q = qp_ref[h]
        s1 = lax.dot_general(k1_ref[h], q[:, :PEER_HALF], _NT, preferred_element_type=F32)
        s2 = lax.dot_general(k2_ref[h], q[:, PEER_HALF:], _NT, preferred_element_type=F32)
        v1 = _top16_sorted(s1)
        v2 = _top16_sorted(s2)
        z, t = _threshold_stats(v1, v2)
        c1 = jnp.full_like(s1, jnp.inf)
        for tj, v in zip(t, v2):
            c1 = jnp.where(s1 >= jnp.tile(tj, (reps, 1)), jnp.tile(v, (reps, 1)), c1)
        c1_ref[0, h] = c1
        s2_ref[0, h] = s2
        a1_ref[0, h] = jnp.exp(s1 - v1[0][0:1, :]) / z[0:1, :]
        b2_ref[0, h] = jnp.exp(s2 - v2[0][0:1, :])

    def two_heads(i, carry):
        head(2 * i)
        head(2 * i + 1)
        return carry

    lax.fori_loop(0, PEER_HEADS // 2, two_heads, 0)


def _topk(qp3, k1, k2):
    nh, t, qd = qp3.shape
    nc = t // LANES
    big = lambda: pl.BlockSpec((1, nh, PEER_KEYS, LANES), lambda i: (i, 0, 0, 0))
    keys = lambda: pl.BlockSpec((nh, PEER_KEYS, PEER_HALF), lambda i: (0, 0, 0))
    sds = jax.ShapeDtypeStruct((nc, nh, PEER_KEYS, LANES), F32)
    return pl.pallas_call(
        _topk_kernel,
        grid=(nc,),
        in_specs=[pl.BlockSpec((nh, LANES, qd), lambda i: (0, i, 0)), keys(), keys()],
        out_specs=[big(), big(), big(), big()],
        out_shape=[sds, sds, sds, sds],
        compiler_params=_params(("arbitrary",)),
        name="topk",
    )(qp3, k1, k2)


PEER_TILE = SUBLANES * PEER_KEYS


def _peer_kernel(ht_ref, u_ref, vt_ref, c1_ref, a1_ref, s2_ref, b2_ref, x1_ref, g2_ref, gf_ref,
                 o_ref, acc, st, act, *, nc):
    j = pl.program_id(1)

    @pl.when(j == 0)
    def _():
        acc[...] = jnp.zeros_like(acc)

    @pl.when(j >= 0)
    def _():
        st[...] = jnp.dot(u_ref[...], ht_ref[...], preferred_element_type=F32)

    @pl.when(j >= -1)
    def _():
        for c in range(nc):
            cs = slice(c * LANES, (c + 1) * LANES)
            for r in range(SUBLANES):
                w = None
                for h in range(PEER_HEADS):
                    sel = jnp.where(s2_ref[c, h] >= c1_ref[c, h, r:r + 1, :], b2_ref[c, h], 0.0)
                    term = sel * a1_ref[c, h, r:r + 1, :]
                    w = term if w is None else w + term
                rows = slice(r * PEER_KEYS, (r + 1) * PEER_KEYS)
                act[rows, cs] = (_gelu_tanh(st[rows, cs]) * w).astype(BF16)

    @pl.when(j >= -2)
    def _():
        acc[...] += jnp.dot(vt_ref[0], act[...], preferred_element_type=F32)

    @pl.when(j == pl.num_programs(1) - 1)
    def _():
        o_ref[...] = _rms(x1_ref[...] + g2_ref[0] * acc[...].T) * gf_ref[...]


def _peer(h2t, u_b, v_tiles, c1, a1, s2, b2, x1, ga2, g_final, seq, tm=512):
    d, t = h2t.shape
    te = PEER_TILE
    ne = u_b.shape[0] // te
    nh = c1.shape[1]
    nc = tm // LANES
    nb = seq // tm
    rows = lambda: pl.BlockSpec((nc, nh, SUBLANES, LANES), lambda i, j: (i, 0, j, 0))
    full = lambda: pl.BlockSpec((nc, nh, PEER_KEYS, LANES), lambda i, j: (i, 0, 0, 0),
                                pipeline_mode=pl.Buffered(1))
    return pl.pallas_call(
        functools.partial(_peer_kernel, nc=nc),
        grid=(t // tm, ne),
        in_specs=[pl.BlockSpec((d, tm), lambda i, j: (0, i)),
                  pl.BlockSpec((te, d), lambda i, j: (j, 0)),
                  pl.BlockSpec((1, d, te), lambda i, j: (j, 0, 0)),
                  rows(), rows(), full(), full(),
                  pl.BlockSpec((tm, d), lambda i, j: (i, 0)),
                  pl.BlockSpec((1, 1, d), lambda i, j: (i // nb, 0, 0)),
                  pl.BlockSpec((1, d), lambda i, j: (0, 0))],
        out_specs=pl.BlockSpec((tm, d), lambda i, j: (i, 0)),
        out_shape=jax.ShapeDtypeStruct((t, d), F32),
        scratch_shapes=[pltpu.VMEM((d, tm), F32), pltpu.VMEM((te, tm), F32),
                        pltpu.VMEM((te, tm), BF16)],
        compiler_params=_params(("arbitrary", "arbitrary")),
        name="peer",
    )(h2t, u_b, v_tiles, c1, a1, s2, b2, x1, ga2, g_final.reshape(1, d))


def kernel(x, c, w_ada, b_ada, g_mix, w_in, b_f, conv_w, conv_b, lru_w_a, lru_b_a, lru_w_x,
           lru_b_x, lru_lambda, g_attn_out, g_lru_out, w_out, g_ffn, peer_w_q, peer_k1,
           peer_k2, peer_u, peer_v, g_final):
    bsz, seq, d = x.shape
    depth = w_ada.shape[0]
    assert depth == 1, "the final residual + norm is fused into the PEER kernel of a single layer"
    x2 = x.reshape(bsz * seq, d)
    off_f = 3 * ATTN_WIDTH
    off_l = off_f + ATTN_HEADS

    for l in range(depth):
        mod = _ada(c, w_ada[l], b_ada[l])
        sh1, sc1, ga1, sh2, sc2, ga2 = [m.reshape(bsz, 1, d) for m in jnp.split(mod, N_ADA, axis=-1)]

        h = _modnorm(x2, g_mix[l], sc1, sh1, seq)
        w_l = w_in[l]
        q_scale = jnp.where(jnp.arange(2 * ATTN_WIDTH) < ATTN_WIDTH, LOG2E * HEAD_DIM ** -0.5, 1.0)
        qk = _matmul_colscale(h, w_l[:, :2 * ATTN_WIDTH].astype(BF16), q_scale.astype(F32), BF16,
                              name="proj_qk")
        vt = _matmul_t(h, w_l[:, 2 * ATTN_WIDTH:off_f].T.astype(BF16), BF16, tm=ATTN_TQ,
                       name="proj_vt")
        w_f = jnp.pad(w_l[:, off_f:off_l], ((0, 0), (0, LANES - ATTN_HEADS)))
        w_lf = jnp.concatenate([w_l[:, off_l:], w_f], axis=1).astype(BF16)
        lxy = _matmul(h, w_lf, F32, tn=w_lf.shape[1], name="proj_lru")

        bf = jnp.pad(b_f[l], (0, LANES - ATTN_HEADS)).reshape(1, LANES)
        cum_rep, cum_row = _cum(lxy, 2 * LRU_WIDTH // LANES, bf, bsz, seq)
        attn = _attn(qk, vt, cum_rep, cum_row, bsz, seq)
        lru_n = _lru(lxy, conv_w[l], conv_b[l], lru_w_a[l], lru_b_a[l], lru_w_x[l], lru_b_x[l],
                     lru_lambda[l], g_lru_out[l], bsz, seq)

        w_o = w_out[l].astype(BF16)
        x1, h2t, qp3 = _outproj(attn, lru_n, x2, g_attn_out[l], w_o[:ATTN_WIDTH],
                                w_o[ATTN_WIDTH:], ga1, sc2, sh2, g_ffn[l],
                                peer_w_q[l].astype(BF16), PEER_HEADS, seq)
        c1, a1, s2, b2 = _topk(qp3, peer_k1[l].astype(BF16), peer_k2[l].astype(BF16))
        v_tiles = peer_v[l].reshape(-1, PEER_TILE, d).transpose(0, 2, 1).astype(BF16)
        out = _peer(h2t, peer_u[l].astype(BF16), v_tiles, c1, a1, s2, b2, x1, ga2, g_final, seq)
    return out.reshape(bsz, seq, d)
```
